```python
import math
import jax, jax.numpy as jnp
from jax import lax
import numpy as np

D_MODEL = 1024
BATCH = 8
SEQ = 8192
DEPTH = 1
DEC_BATCH = 128
DEC_SEQ = 4
PAST_LEN = 8192
PAGE_SIZE = 128

N_HEADS_A = 8
HEAD_DIM_A = 64
V_DIM_A = 2 * HEAD_DIM_A
SCALE_A = HEAD_DIM_A ** -0.5
Q_BLOCK = 128
N_HEADS_B = 4
DK_B = (D_MODEL // 2) // N_HEADS_B
DV_B = D_MODEL // N_HEADS_B
ALPHA_RANK = 16
GLA_TAU = 16.0
GLA_CHUNK = 64
FFN_HIDDEN = -(-8 * D_MODEL // (3 * 256)) * 256
EPS = 1e-6

QA_W = N_HEADS_A * 2 * HEAD_DIM_A
KA_W = N_HEADS_A * 2 * HEAD_DIM_A
VA_W = N_HEADS_A * V_DIM_A
QB_W = N_HEADS_B * DK_B
KB_W = N_HEADS_B * DK_B
VB_W = N_HEADS_B * DV_B
GATE_W = 2 * D_MODEL
IN_WIDTH = QA_W + KA_W + VA_W + QB_W + KB_W + VB_W + ALPHA_RANK + GATE_W
SPLIT_IDX = list(np.cumsum([QA_W, KA_W, VA_W, QB_W, KB_W, VB_W, ALPHA_RANK]))

kernel_name = "hybrid_diffattn_gla_gated_step"


def _rms(x, w):
    xf = x.astype(jnp.float32)
    y = xf * lax.rsqrt(jnp.mean(xf * xf, axis=-1, keepdims=True) + EPS)
    return (y * w.astype(jnp.float32)).astype(x.dtype)


def _alibi_slopes(n):
    return jnp.asarray(np.array([2.0 ** (-8.0 * (i + 1) / n) for i in range(n)], np.float32))


def _alibi_bias(qpos, kpos, slopes):
    d = (qpos[:, None] - kpos[None, :]).astype(jnp.float32)
    return jnp.where(d >= 0, -slopes[:, None, None] * d, -jnp.inf)


def _front(h, w_in, w_alpha_up, b_alpha, q_norm_w, k_norm_w):
    B, T = h.shape[:2]
    z = jnp.einsum('btd,de->bte', h, w_in)
    qa, ka, va, qb, kb, vb, a_low, g = jnp.split(z, SPLIT_IDX, axis=-1)
    qa = _rms(qa.reshape(B, T, N_HEADS_A, 2, HEAD_DIM_A), q_norm_w)
    ka = _rms(ka.reshape(B, T, N_HEADS_A, 2, HEAD_DIM_A), k_norm_w).reshape(B, T, N_HEADS_A, 2 * HEAD_DIM_A)
    va = va.reshape(B, T, N_HEADS_A, V_DIM_A)
    qb = qb.reshape(B, T, N_HEADS_B, DK_B)
    kb = kb.reshape(B, T, N_HEADS_B, DK_B)
    vb = vb.reshape(B, T, N_HEADS_B, DV_B)
    logits = (jnp.einsum('btr,re->bte', a_low, w_alpha_up) + b_alpha).astype(jnp.float32)
    log_a = (jax.nn.log_sigmoid(logits) / GLA_TAU).reshape(B, T, N_HEADS_B, DK_B)
    gates = jax.nn.sigmoid(g)
    return qa, ka, va, qb, kb, vb, log_a, gates


def _lambda(lq1, lk1, lq2, lk2, lam_init):
    f = jnp.float32
    return (jnp.exp(jnp.sum(lq1.astype(f) * lk1.astype(f))) - jnp.exp(jnp.sum(lq2.astype(f) * lk2.astype(f)))
            + lam_init)


def _diff_attn_prompt(q, k, v, lam):
    B, S = q.shape[:2]
    nb = S // Q_BLOCK
    k = k.reshape(B, S, N_HEADS_A, 2, HEAD_DIM_A)
    kpos = jnp.arange(S)
    slopes = _alibi_slopes(N_HEADS_A)
    qblocks = q.reshape(B, nb, Q_BLOCK, N_HEADS_A, 2, HEAD_DIM_A).swapaxes(0, 1)

    def one_block(args):
        i, qi = args
        qpos = i * Q_BLOCK + jnp.arange(Q_BLOCK)
        s = jnp.einsum('bqhmd,bkhmd->bmhqk', qi, k, preferred_element_type=jnp.float32) * SCALE_A
        s = s + _alibi_bias(qpos, kpos, slopes)
        p = jax.nn.softmax(s, axis=-1)
        w = p[:, 0] - lam * p[:, 1]
        return jnp.einsum('bhqk,bkhv->bqhv', w.astype(v.dtype), v)

    o = lax.map(one_block, (jnp.arange(nb), qblocks))
    return o.swapaxes(0, 1).reshape(B, S, N_HEADS_A, V_DIM_A)


def _diff_attn_sample(q, k_new, v_new, past_k, past_v, lam):
    B, T = q.shape[:2]
    P = past_k.shape[1]
    slopes = _alibi_slopes(N_HEADS_A)
    qpos = P + jnp.arange(T)
    kp = past_k.reshape(B, P, N_HEADS_A, 2, HEAD_DIM_A)
    kn = k_new.reshape(B, T, N_HEADS_A, 2, HEAD_DIM_A)
    s_past = jnp.einsum('bqhmd,bkhmd->bmhqk', q, kp, preferred_element_type=jnp.float32) * SCALE_A
    s_past = s_past + _alibi_bias(qpos, jnp.arange(P), slopes)
    s_new = jnp.einsum('bqhmd,bkhmd->bmhqk', q, kn, preferred_element_type=jnp.float32) * SCALE_A
    s_new = s_new + _alibi_bias(qpos, P + jnp.arange(T), slopes)
    p = jax.nn.softmax(jnp.concatenate([s_past, s_new], axis=-1), axis=-1)
    w = (p[:, 0] - lam * p[:, 1]).astype(v_new.dtype)
    return (jnp.einsum('bhqk,bkhv->bqhv', w[..., :P], past_v)
            + jnp.einsum('bhqk,bkhv->bqhv', w[..., P:], v_new))


def _gla(q, k, v, log_a, s0):
    B, T, H, dk = q.shape
    dv = v.shape[-1]
    C = math.gcd(T, GLA_CHUNK)
    nc = T // C
    f = jnp.float32

    def chunks(a):
        return a.astype(f).reshape(B, nc, C, *a.shape[2:]).swapaxes(0, 1)

    tri = jnp.tril(jnp.ones((C, C), bool))

    def step(S, inp):
        qc, kc, vc, lac = inp
        cum = jnp.cumsum(lac, axis=1)
        last = cum[:, -1]
        qd = qc * jnp.exp(cum)
        att = jnp.einsum('bthk,bshk->bhts', qd, kc * jnp.exp(-cum))
        att = jnp.where(tri, att, 0.0)
        o = jnp.einsum('bhts,bshv->bthv', att, vc) + jnp.einsum('bthk,bhkv->bthv', qd, S)
        S = S * jnp.exp(last)[..., None] + jnp.einsum('bshk,bshv->bhkv', kc * jnp.exp(last[:, None] - cum), vc)
        return S, o

    S, o = lax.scan(step, s0.astype(f), (chunks(q * dk ** -0.5), chunks(k), chunks(v), chunks(log_a)))
    o = o.swapaxes(0, 1).reshape(B, T, H, dv)
    return o.astype(v.dtype), S.astype(s0.dtype)


def _back(x, oa, ob, gates, lam_init, diff_norm_w, gla_norm_w, w_out, ffn_norm_w, w_gate, w_up, w_down):
    B, T = x.shape[:2]
    oa = (_rms(oa, diff_norm_w) * (1.0 - lam_init)).reshape(B, T, D_MODEL)
    ob = _rms(ob, gla_norm_w).reshape(B, T, D_MODEL)
    mix = gates[..., :D_MODEL] * oa + gates[..., D_MODEL:] * ob
    x = x + jnp.einsum('btd,de->bte', mix, w_out)
    h2 = _rms(x, ffn_norm_w)
    u = jax.nn.silu(jnp.einsum('btd,df->btf', h2, w_gate)) * jnp.einsum('btd,df->btf', h2, w_up)
    return x + jnp.einsum('btf,fd->btd', u, w_down)


def setup_inputs(seed: int = 0) -> dict:
    key = jax.random.key(seed)
    ks = jax.random.split(key, 32)
    f = jnp.float32
    n_pages = PAST_LEN // PAGE_SIZE
    n_used = DEC_BATCH * n_pages
    n_pool = n_used + max(1, n_used // 4)

    def nrm(k, shape, scale):
        return jax.random.normal(k, shape, f) * scale

    def gain(k, shape):
        return 1.0 + 0.02 * jax.random.normal(k, shape, f)

    page_table = jax.random.permutation(ks[0], n_pool)[:n_used].reshape(DEC_BATCH, n_pages).astype(jnp.int32)
    return {
        "x_prompt": nrm(ks[1], (BATCH, SEQ, D_MODEL), 1.0),
        "x_sample": nrm(ks[2], (DEC_BATCH, DEC_SEQ, D_MODEL), 1.0),
        "cache_k": nrm(ks[3], (DEPTH, n_pool, PAGE_SIZE, N_HEADS_A, 2 * HEAD_DIM_A), 1.0),
        "cache_v": nrm(ks[4], (DEPTH, n_pool, PAGE_SIZE, N_HEADS_A, V_DIM_A), 1.0),
        "state_gla": nrm(ks[5], (DEPTH, DEC_BATCH, N_HEADS_B, DK_B, DV_B), 2.0),
        "page_table": page_table,
        "attn_norm_w": gain(ks[6], (DEPTH, D_MODEL)),
        "w_in": nrm(ks[7], (DEPTH, D_MODEL, IN_WIDTH), D_MODEL ** -0.5),
        "w_alpha_up": nrm(ks[8], (DEPTH, ALPHA_RANK, N_HEADS_B * DK_B), ALPHA_RANK ** -0.5),
        "b_alpha": nrm(ks[9], (DEPTH, N_HEADS_B * DK_B), 0.1),
        "q_norm_w": gain(ks[10], (DEPTH, HEAD_DIM_A)),
        "k_norm_w": gain(ks[11], (DEPTH, HEAD_DIM_A)),
        "lambda_q1": nrm(ks[12], (DEPTH, HEAD_DIM_A), 0.1),
        "lambda_k1": nrm(ks[13], (DEPTH, HEAD_DIM_A), 0.1),
        "lambda_q2": nrm(ks[14], (DEPTH, HEAD_DIM_A), 0.1),
        "lambda_k2": nrm(ks[15], (DEPTH, HEAD_DIM_A), 0.1),
        "diff_norm_w": gain(ks[16], (DEPTH, V_DIM_A)),
        "gla_norm_w": gain(ks[17], (DEPTH, DV_B)),
        "w_out": nrm(ks[18], (DEPTH, D_MODEL, D_MODEL), D_MODEL ** -0.5),
        "ffn_norm_w": gain(ks[19], (DEPTH, D_MODEL)),
        "w_ffn_gate": nrm(ks[20], (DEPTH, D_MODEL, FFN_HIDDEN), D_MODEL ** -0.5),
        "w_ffn_up": nrm(ks[21], (DEPTH, D_MODEL, FFN_HIDDEN), D_MODEL ** -0.5),
        "w_ffn_down": nrm(ks[22], (DEPTH, FFN_HIDDEN, D_MODEL), FFN_HIDDEN ** -0.5),
    }


def reference(x_prompt, x_sample, cache_k, cache_v, state_gla, page_table, attn_norm_w, w_in, w_alpha_up,
              b_alpha, q_norm_w, k_norm_w, lambda_q1, lambda_k1, lambda_q2, lambda_k2, diff_norm_w, gla_norm_w,
              w_out, ffn_norm_w, w_ffn_gate, w_ffn_up, w_ffn_down):
    xp, xs = x_prompt, x_sample
    B = xp.shape[0]
    DB = xs.shape[0]
    P = page_table.shape[1] * PAGE_SIZE
    kp_l, vp_l, sp_l, ks_l, vs_l, ss_l = [], [], [], [], [], []
    for l in range(DEPTH):
        lam_init = 0.8 - 0.6 * math.exp(-0.3 * l)
        lam = _lambda(lambda_q1[l], lambda_k1[l], lambda_q2[l], lambda_k2[l], lam_init)
        back = lambda x, oa, ob, g: _back(x, oa, ob, g, lam_init, diff_norm_w[l], gla_norm_w[l], w_out[l],
                                          ffn_norm_w[l], w_ffn_gate[l], w_ffn_up[l], w_ffn_down[l])
        qa, ka, va, qb, kb, vb, la, g = _front(_rms(xp, attn_norm_w[l]), w_in[l], w_alpha_up[l], b_alpha[l],
                                               q_norm_w[l], k_norm_w[l])
        oa = _diff_attn_prompt(qa, ka, va, lam)
        ob, s_p = _gla(qb, kb, vb, la, jnp.zeros((B, N_HEADS_B, DK_B, DV_B), xp.dtype))
        xp = back(xp, oa, ob, g)
        kp_l.append(ka); vp_l.append(va); sp_l.append(s_p)
        qa, ka, va, qb, kb, vb, la, g = _front(_rms(xs, attn_norm_w[l]), w_in[l], w_alpha_up[l], b_alpha[l],
                                               q_norm_w[l], k_norm_w[l])
        past_k = cache_k[l, page_table].reshape(DB, P, N_HEADS_A, 2 * HEAD_DIM_A)
        past_v = cache_v[l, page_table].reshape(DB, P, N_HEADS_A, V_DIM_A)
        oa = _diff_attn_sample(qa, ka, va, past_k, past_v, lam)
        ob, s_s = _gla(qb, kb, vb, la, state_gla[l])
        xs = back(xs, oa, ob, g)
        ks_l.append(ka); vs_l.append(va); ss_l.append(s_s)
    return (xp, xs, jnp.stack(kp_l), jnp.stack(vp_l), jnp.stack(sp_l), jnp.stack(ks_l), jnp.stack(vs_l),
            jnp.stack(ss_l))
```

```python
import functools
import math

import numpy as np
import jax
import jax.numpy as jnp
from jax import lax
from jax.experimental import pallas as pl
from jax.experimental.pallas import tpu as pltpu

D_MODEL = 1024
N_HEADS_A = 8
HEAD_DIM_A = 64
V_DIM_A = 2 * HEAD_DIM_A
SCALE_A = HEAD_DIM_A ** -0.5
N_HEADS_B = 4
DK_B = 128
DV_B = 256
ALPHA_RANK = 16
GLA_TAU = 16.0
GLA_CHUNK = 64
FFN_HIDDEN = 2816
EPS = 1e-6
PAGE_SIZE = 128
LAM_INIT = 0.8 - 0.6 * math.exp(-0.3 * 0)

QA_W = N_HEADS_A * 2 * HEAD_DIM_A
VA_W = N_HEADS_A * V_DIM_A
QB_W = N_HEADS_B * DK_B
VB_W = N_HEADS_B * DV_B
MAIN_W = 3 * QA_W + 2 * QB_W + VB_W
GATE_W = 2 * D_MODEL

LANES = 128
V7X_VMEM_LIMIT_BYTES = 56 * 1024 * 1024

F32 = jnp.float32
BF16 = jnp.bfloat16
NEG_INF = float("-inf")


def _const_spec(shape):
    nd = len(shape)
    return pl.BlockSpec(shape, lambda *_: (0,) * nd, pipeline_mode=pl.Buffered(1))


def _params(semantics):
    return pltpu.CompilerParams(dimension_semantics=semantics, vmem_limit_bytes=V7X_VMEM_LIMIT_BYTES)


def _rms_rows(x, w):
    ms = jnp.mean(x * x, axis=-1, keepdims=True)
    return x * lax.rsqrt(ms + EPS) * w


def _sigmoid(x):
    return 1.0 / (1.0 + jnp.exp(-x))


def _log_sigmoid(x):
    return jnp.minimum(x, 0.0) - jnp.log1p(jnp.exp(-jnp.abs(x)))


def _nt_dot(a, b):
    return lax.dot_general(a, b, (((1,), (1,)), ((), ())), preferred_element_type=F32)


def _map_norm(z, w2):
    lane = lax.broadcasted_iota(jnp.int32, z.shape, 1)
    first = lane < HEAD_DIM_A
    sq = z * z
    s1 = jnp.sum(jnp.where(first, sq, 0.0), axis=-1, keepdims=True)
    s2 = jnp.sum(jnp.where(first, 0.0, sq), axis=-1, keepdims=True)
    ms = jnp.where(first, s1, s2) * (1.0 / HEAD_DIM_A)
    return z * lax.rsqrt(ms + EPS) * w2


def _front_kernel(x_ref, nw_ref, wm_ref, wa_ref, wg_ref, wup_ref, ba_ref, qw_ref, kw_ref,
                  q16_ref, k32_ref, k16_ref, v32_ref, v16_ref, qb_ref, kb_ref, vb_ref, la_ref, g_ref):
    h = _rms_rows(x_ref[...], nw_ref[...]).astype(BF16)

    def proj(w):
        return jnp.dot(h, w, preferred_element_type=F32)

    zq = proj(wm_ref[:, 0:QA_W])
    for hd in range(N_HEADS_A):
        sl = slice(hd * LANES, (hd + 1) * LANES)
        q16_ref[:, sl] = (_map_norm(zq[:, sl], qw_ref[...]) * SCALE_A).astype(BF16)
    zk = proj(wm_ref[:, QA_W:2 * QA_W])
    for hd in range(N_HEADS_A):
        sl = slice(hd * LANES, (hd + 1) * LANES)
        kn = _map_norm(zk[:, sl], kw_ref[...])
        k32_ref[:, sl] = kn
        k16_ref[:, sl] = kn.astype(BF16)
    zv = proj(wm_ref[:, 2 * QA_W:3 * QA_W])
    v32_ref[...] = zv
    v16_ref[...] = zv.astype(BF16)
    o = 3 * QA_W
    qb_ref[...] = proj(wm_ref[:, o:o + QB_W])
    kb_ref[...] = proj(wm_ref[:, o + QB_W:o + 2 * QB_W])
    vb_ref[...] = proj(wm_ref[:, o + 2 * QB_W:o + 2 * QB_W + VB_W])
    a_low = proj(wa_ref[...]).astype(BF16)
    logits = jnp.dot(a_low, wup_ref[...], preferred_element_type=F32) + ba_ref[...]
    la_ref[...] = _log_sigmoid(logits) * (1.0 / GLA_TAU)
    g_ref[...] = _sigmoid(proj(wg_ref[...]))


def _front(x2d, nw, wm, wa, wg, wup, ba, qw2, kw2, tm):
    m = x2d.shape[0]
    row = lambda w: pl.BlockSpec((tm, w), lambda i: (i, 0))
    outs = [(QA_W, BF16), (QA_W, F32), (QA_W, BF16), (VA_W, F32), (VA_W, BF16),
            (QB_W, F32), (QB_W, F32), (VB_W, F32), (QB_W, F32), (GATE_W, F32)]
    return pl.pallas_call(
        _front_kernel,
        grid=(m // tm,),
        in_specs=[row(D_MODEL), _const_spec(nw.shape), _const_spec(wm.shape), _const_spec(wa.shape),
                  _const_spec(wg.shape), _const_spec(wup.shape), _const_spec(ba.shape),
                  _const_spec(qw2.shape), _const_spec(kw2.shape)],
        out_specs=[row(w) for w, _ in outs],
        out_shape=[jax.ShapeDtypeStruct((m, w), dt) for w, dt in outs],
        compiler_params=_params(("parallel",)),
        name="front",
    )(x2d, nw, wm, wa, wg, wup, ba, qw2, kw2)


def _lambda_from(lv_ref):
    lv = lv_ref[...]
    a = jnp.sum(lv[0:1] * lv[1:2], axis=-1, keepdims=True)
    b = jnp.sum(lv[2:3] * lv[3:4], axis=-1, keepdims=True)
    return jnp.exp(a) - jnp.exp(b) + LAM_INIT


def _online_softmax_step(s, v16, m_ref, l_ref, acc_ref):
    m_old = m_ref[...]
    m_new = jnp.maximum(m_old, jnp.max(s, axis=-1, keepdims=True))
    alpha = jnp.exp(m_old - m_new)
    p = jnp.exp(s - m_new)
    l_ref[...] = alpha * l_ref[...] + jnp.sum(p, axis=-1, keepdims=True)
    acc_ref[...] = alpha * acc_ref[...] + jnp.dot(p.astype(BF16), v16, preferred_element_type=F32)
    m_ref[...] = m_new


def _attn_prompt_kernel(lv_ref, sl_ref, q_ref, k_ref, v_ref, o_ref, qs_ref, m_ref, l_ref, acc_ref, *, tq, tk):
    i = pl.program_id(2)
    j = pl.program_id(3)
    last_j = (i * tq + tq - 1) // tk

    @pl.when(j == 0)
    def _():
        q = q_ref[0]
        lane = lax.broadcasted_iota(jnp.int32, q.shape, 1)
        zero = jnp.zeros_like(q)
        qs_ref[0:tq, :] = jnp.where(lane < HEAD_DIM_A, q, zero)
        qs_ref[tq:2 * tq, :] = jnp.where(lane < HEAD_DIM_A, zero, q)
        m_ref[...] = jnp.full(m_ref.shape, NEG_INF, F32)
        l_ref[...] = jnp.zeros(l_ref.shape, F32)
        acc_ref[...] = jnp.zeros(acc_ref.shape, F32)

    def step(masked):
        s = _nt_dot(qs_ref[...], k_ref[0])
        off = (j * tk - i * tq).astype(F32)
        s = s + (sl_ref[0, 0:1, :] + sl_ref[0, 1:2, :] * off)
        if masked:
            r = lax.broadcasted_iota(jnp.int32, s.shape, 0)
            r = jnp.where(r >= tq, r - tq, r) + i * tq
            c = lax.broadcasted_iota(jnp.int32, s.shape, 1) + j * tk
            s = jnp.where(c <= r, s, NEG_INF)
        _online_softmax_step(s, v_ref[0], m_ref, l_ref, acc_ref)

    fully_visible = (j * tk + tk - 1) <= (i * tq)

    @pl.when(fully_visible)
    def _():
        step(False)

    @pl.when(jnp.logical_and(j <= last_j, jnp.logical_not(fully_visible)))
    def _():
        step(True)

    @pl.when(j == last_j)
    def _():
        lam = _lambda_from(lv_ref)
        o1 = acc_ref[0:tq, :] / l_ref[0:tq, :]
        o2 = acc_ref[tq:2 * tq, :] / l_ref[tq:2 * tq, :]
        o_ref[0] = o1 - lam * o2


def _attn_prompt(lv, slope_rows, q16, k16, v16, tq, tk):
    b, s, _ = q16.shape
    nq, nk = s // tq, s // tk

    def kv_map(bi, h, i, j):
        return (bi, jnp.minimum(j, (i * tq + tq - 1) // tk), h)

    return pl.pallas_call(
        functools.partial(_attn_prompt_kernel, tq=tq, tk=tk),
        grid=(b, N_HEADS_A, nq, nk),
        in_specs=[pl.BlockSpec(lv.shape, lambda bi, h, i, j: (0, 0)),
                  pl.BlockSpec((1, 2, tk), lambda bi, h, i, j: (h, 0, 0)),
                  pl.BlockSpec((1, tq, LANES), lambda bi, h, i, j: (bi, i, h)),
                  pl.BlockSpec((1, tk, LANES), kv_map),
                  pl.BlockSpec((1, tk, LANES), kv_map)],
        out_specs=pl.BlockSpec((1, tq, LANES), lambda bi, h, i, j: (bi, i, h)),
        out_shape=jax.ShapeDtypeStruct((b, s, VA_W), F32),
        scratch_shapes=[pltpu.VMEM((2 * tq, LANES), BF16),
                        pltpu.VMEM((2 * tq, 1), F32),
                        pltpu.VMEM((2 * tq, 1), F32),
                        pltpu.VMEM((2 * tq, V_DIM_A), F32)],
        compiler_params=_params(("parallel", "parallel", "parallel", "arbitrary")),
        name="attn_prompt",
    )(lv, slope_rows, q16, k16, v16)


COLS_S = 128
NEW_PAD = 8


def _attn_sample_kernel(pt_ref, lv_ref, hv_ref, q_ref, kn_ref, vn_ref, *rest, gp, t_new, past_len):
    k_refs = rest[:gp]
    v_refs = rest[gp:2 * gp]
    o_ref, qr_ref, m_ref, l_ref, acc_ref = rest[2 * gp:]
    p = pl.program_id(1)
    slope = hv_ref[0]
    qpos = hv_ref[1] + float(past_len)
    own = hv_ref[2] > 0.5

    @pl.when(p == 0)
    def _():
        lane = lax.broadcasted_iota(jnp.int32, (N_HEADS_A, LANES), 1)
        tiles = []
        for t in range(t_new):
            qt = q_ref[0, t]
            tiles.append(jnp.where(lane < HEAD_DIM_A, qt, 0.0))
            tiles.append(jnp.where(lane < HEAD_DIM_A, 0.0, qt))
        tiles.append(jnp.zeros((COLS_S - 16 * t_new, LANES), F32))
        qr_ref[...] = jnp.concatenate(tiles, axis=0).astype(BF16)
        m_ref[...] = jnp.full(m_ref.shape, NEG_INF, F32)
        l_ref[...] = jnp.zeros(l_ref.shape, F32)
        acc_ref[...] = jnp.zeros(acc_ref.shape, F32)

    def update(k2, v2, kpos, mask):
        n = k2.shape[0] // N_HEADS_A
        s = _nt_dot(k2, qr_ref[...]).reshape(n, N_HEADS_A, COLS_S)
        s = s - slope * (qpos - kpos)
        if mask is not None:
            s = jnp.where(mask, s, NEG_INF)
        m_old = m_ref[...]
        m_new = jnp.maximum(m_old, jnp.max(s, axis=0))
        alpha = jnp.exp(m_old - m_new)
        pr = jnp.exp(s - m_new)
        l_ref[...] = alpha * l_ref[...] + jnp.sum(pr, axis=0)
        m_ref[...] = m_new
        pm = jnp.where(own, pr, 0.0).reshape(n * N_HEADS_A, COLS_S).astype(BF16)
        pv = lax.dot_general(v2, pm, (((0,), (0,)), ((), ())), preferred_element_type=F32)
        alpha_row = jnp.sum(jnp.where(own, alpha, 0.0), axis=0, keepdims=True)
        acc_ref[...] = acc_ref[...] * alpha_row + pv

    n_keys = gp * PAGE_SIZE
    rows = lambda r: r[0].reshape(PAGE_SIZE * N_HEADS_A, LANES).astype(BF16)
    k2 = jnp.concatenate([rows(r) for r in k_refs], axis=0)
    v2 = jnp.concatenate([rows(r) for r in v_refs], axis=0)
    kpos = (lax.broadcasted_iota(jnp.int32, (n_keys, N_HEADS_A, COLS_S), 0) + p * n_keys).astype(F32)
    update(k2, v2, kpos, None)

    @pl.when(p == pl.num_programs(1) - 1)
    def _():
        pos = lax.broadcasted_iota(jnp.int32, (NEW_PAD, N_HEADS_A, COLS_S), 0)
        kposn = pos.astype(F32) + float(past_len)
        mask = jnp.logical_and(pos < t_new, kposn <= qpos)
        new_rows = lambda r: r[0].reshape(NEW_PAD * N_HEADS_A, LANES).astype(BF16)
        update(new_rows(kn_ref), new_rows(vn_ref), kposn, mask)
        lam = _lambda_from(lv_ref)
        l_row = jnp.sum(jnp.where(own, l_ref[...], 0.0), axis=0, keepdims=True)
        used = jnp.sum(jnp.where(own, 1.0, 0.0), axis=0, keepdims=True) > 0.5
        o = (acc_ref[...] / jnp.where(used, l_row, 1.0)).T
        for t in range(t_new):
            o_ref[0, t] = o[16 * t:16 * t + 8, :] - lam * o[16 * t + 8:16 * t + 16, :]


def _attn_sample(page_table, lv, head_info, q, kn, vn, cache_k, cache_v, gp):
    db, t_new = q.shape[:2]
    n_pages = page_table.shape[1]
    past_len = n_pages * PAGE_SIZE
    blk4 = lambda n: (1, n, N_HEADS_A, LANES)

    def page_spec(g):
        return pl.BlockSpec(blk4(PAGE_SIZE), lambda b, p, pt: (pt[b, p * gp + g], 0, 0, 0))

    grid_spec = pltpu.PrefetchScalarGridSpec(
        num_scalar_prefetch=1,
        grid=(db, n_pages // gp),
        in_specs=[pl.BlockSpec(lv.shape, lambda b, p, pt: (0, 0)),
                  pl.BlockSpec(head_info.shape, lambda b, p, pt: (0, 0, 0)),
                  pl.BlockSpec(blk4(t_new), lambda b, p, pt: (b, 0, 0, 0)),
                  pl.BlockSpec(blk4(NEW_PAD), lambda b, p, pt: (b, 0, 0, 0)),
                  pl.BlockSpec(blk4(NEW_PAD), lambda b, p, pt: (b, 0, 0, 0))]
                 + [page_spec(g) for g in range(gp)] + [page_spec(g) for g in range(gp)],
        out_specs=pl.BlockSpec(blk4(t_new), lambda b, p, pt: (b, 0, 0, 0)),
        scratch_shapes=[pltpu.VMEM((COLS_S, LANES), BF16),
                        pltpu.VMEM((N_HEADS_A, COLS_S), F32),
                        pltpu.VMEM((N_HEADS_A, COLS_S), F32),
                        pltpu.VMEM((V_DIM_A, COLS_S), F32)],
    )
    return pl.pallas_call(
        functools.partial(_attn_sample_kernel, gp=gp, t_new=t_new, past_len=past_len),
        grid_spec=grid_spec,
        out_shape=jax.ShapeDtypeStruct((db, t_new, N_HEADS_A, V_DIM_A), F32),
        compiler_params=_params(("parallel", "arbitrary")),
        name="attn_sample",
    )(page_table, lv, head_info, q, kn, vn, *([cache_k] * gp), *([cache_v] * gp))


def _split3(x):
    hi = x.astype(BF16)
    r1 = x - hi.astype(F32)
    mid = r1.astype(BF16)
    lo = (r1 - mid.astype(F32)).astype(BF16)
    return hi, mid, lo


def _gla_kernel(q_ref, k_ref, v_ref, la_ref, s0_ref, o_ref, sout_ref, st_ref, *, chunk):
    c = pl.program_id(1)

    @pl.when(c == 0)
    def _():
        for h in range(N_HEADS_B):
            st_ref[h] = s0_ref[0, h].T

    row = lax.broadcasted_iota(jnp.int32, (chunk, chunk), 0)
    col = lax.broadcasted_iota(jnp.int32, (chunk, chunk), 1)
    tri = row >= col
    tri16 = jnp.where(tri, 1.0, 0.0).astype(BF16)

    for h in range(N_HEADS_B):
        ks = slice(h * DK_B, (h + 1) * DK_B)
        vs = slice(h * DV_B, (h + 1) * DV_B)
        q = q_ref[0, :, ks] * (DK_B ** -0.5)
        k = k_ref[0, :, ks]
        v = v_ref[0, :, vs]
        hi, mid, lo = _split3(la_ref[0, :, ks])
        cum = (jnp.dot(tri16, hi, preferred_element_type=F32)
               + jnp.dot(tri16, mid, preferred_element_type=F32)
               + jnp.dot(tri16, lo, preferred_element_type=F32))
        last = cum[chunk - 1:chunk, :]
        qd = (q * jnp.exp(cum)).astype(BF16)
        kd = (k * jnp.exp(-cum)).astype(BF16)
        k2 = k * jnp.exp(last - cum)
        att = jnp.where(tri, _nt_dot(qd, kd), 0.0)
        st = st_ref[h]
        v16 = v.astype(BF16)
        o = jnp.dot(att.astype(BF16), v16, preferred_element_type=F32) + _nt_dot(qd, st.astype(BF16))
        o_ref[0, :, vs] = o
        st_ref[h] = st * jnp.exp(last) + jnp.dot(v.T.astype(BF16), k2.astype(BF16), preferred_element_type=F32)

    @pl.when(c == pl.num_programs(1) - 1)
    def _():
        for h in range(N_HEADS_B):
            sout_ref[0, h] = st_ref[h].T


def _gla(qb, kb, vb, la, s0, chunk):
    b, t, _ = qb.shape
    nc = t // chunk
    tok = lambda w: pl.BlockSpec((1, chunk, w), lambda bi, c: (bi, c, 0))
    st_spec = pl.BlockSpec((1, N_HEADS_B, DK_B, DV_B), lambda bi, c: (bi, 0, 0, 0))
    return pl.pallas_call(
        functools.partial(_gla_kernel, chunk=chunk),
        grid=(b, nc),
        in_specs=[tok(QB_W), tok(QB_W), tok(VB_W), tok(QB_W), st_spec],
        out_specs=[tok(VB_W), st_spec],
        out_shape=[jax.ShapeDtypeStruct((b, t, VB_W), F32),
                   jax.ShapeDtypeStruct((b, N_HEADS_B, DK_B, DV_B), F32)],
        scratch_shapes=[pltpu.VMEM((N_HEADS_B, DV_B, DK_B), F32)],
        compiler_params=_params(("parallel", "arbitrary")),
        name="gla",
    )(qb, kb, vb, la, s0)


def _back_kernel(x_ref, oa_ref, ob_ref, g_ref, dn_ref, gn_ref, wo_ref, fn_ref, wg_ref, wu_ref, wd_ref, y_ref):
    parts = []
    for h in range(N_HEADS_A):
        sl = slice(h * V_DIM_A, (h + 1) * V_DIM_A)
        oa = _rms_rows(oa_ref[:, sl], dn_ref[...]) * (1.0 - LAM_INIT)
        parts.append(g_ref[:, sl] * oa)
    mix_a = jnp.concatenate(parts, axis=-1)
    parts = []
    for h in range(N_HEADS_B):
        sl = slice(h * DV_B, (h + 1) * DV_B)
        ob = _rms_rows(ob_ref[:, sl], gn_ref[...])
        parts.append(g_ref[:, D_MODEL + h * DV_B:D_MODEL + (h + 1) * DV_B] * ob)
    mix = (mix_a + jnp.concatenate(parts, axis=-1)).astype(BF16)
    x1 = x_ref[...] + jnp.dot(mix, wo_ref[...], preferred_element_type=F32)
    h2 = _rms_rows(x1, fn_ref[...]).astype(BF16)
    gt = jnp.dot(h2, wg_ref[...], preferred_element_type=F32)
    up = jnp.dot(h2, wu_ref[...], preferred_element_type=F32)
    u = (gt * _sigmoid(gt) * up).astype(BF16)
    y_ref[...] = x1 + jnp.dot(u, wd_ref[...], preferred_element_type=F32)


def _back(x2d, oa, ob, g, dn, gn, wo, fn, wg, wu, wd, tm):
    m = x2d.shape[0]
    row = lambda w: pl.BlockSpec((tm, w), lambda i: (i, 0))
    return pl.pallas_call(
        _back_kernel,
        grid=(m // tm,),
        in_specs=[row(D_MODEL), row(VA_W), row(VB_W), row(GATE_W)]
                 + [_const_spec(a.shape) for a in (dn, gn, wo, fn, wg, wu, wd)],
        out_specs=row(D_MODEL),
        out_shape=jax.ShapeDtypeStruct((m, D_MODEL), F32),
        compiler_params=_params(("parallel",)),
        name="back",
    )(x2d, oa, ob, g, dn, gn, wo, fn, wg, wu, wd)


def _alibi_slopes(n):
    return np.array([2.0 ** (-8.0 * (i + 1) / n) for i in range(n)], np.float32)


def _pick_tile(n, pref):
    t = min(pref, n)
    assert n % t == 0, (n, t)
    return t


def kernel(x_prompt, x_sample, cache_k, cache_v, state_gla, page_table, attn_norm_w, w_in, w_alpha_up, b_alpha,
           q_norm_w, k_norm_w, lambda_q1, lambda_k1, lambda_q2, lambda_k2, diff_norm_w, gla_norm_w, w_out,
           ffn_norm_w, w_ffn_gate, w_ffn_up, w_ffn_down):
    depth = w_in.shape[0]
    assert depth == 1, "single-layer step"
    b, s, _ = x_prompt.shape
    db, t_new, _ = x_sample.shape
    n_pool = cache_k.shape[1]
    assert 16 * t_new <= COLS_S and t_new <= NEW_PAD

    w_in0 = w_in[0]
    wm = w_in0[:, :MAIN_W].astype(BF16)
    wa = jnp.pad(w_in0[:, MAIN_W:MAIN_W + ALPHA_RANK], ((0, 0), (0, LANES - ALPHA_RANK))).astype(BF16)
    wg_in = w_in0[:, MAIN_W + ALPHA_RANK:].astype(BF16)
    wup = jnp.pad(w_alpha_up[0], ((0, LANES - ALPHA_RANK), (0, 0))).astype(BF16)
    ba = b_alpha[0][None, :]
    nw = attn_norm_w[0][None, :]
    qw2 = jnp.tile(q_norm_w[0], 2)[None, :]
    kw2 = jnp.tile(k_norm_w[0], 2)[None, :]
    lv = jnp.stack([lambda_q1[0], lambda_k1[0], lambda_q2[0], lambda_k2[0]])
    dn = diff_norm_w[0][None, :]
    gn = gla_norm_w[0][None, :]
    fn = ffn_norm_w[0][None, :]
    wo = w_out[0].astype(BF16)
    wfg = w_ffn_gate[0].astype(BF16)
    wfu = w_ffn_up[0].astype(BF16)
    wfd = w_ffn_down[0].astype(BF16)
    slopes = _alibi_slopes(N_HEADS_A)

    def front(x3d):
        m = x3d.shape[0] * x3d.shape[1]
        return _front(x3d.reshape(m, D_MODEL), nw, wm, wa, wg_in, wup, ba, qw2, kw2, _pick_tile(m, 256))

    def back(x3d, oa, ob, g):
        m = x3d.shape[0] * x3d.shape[1]
        y = _back(x3d.reshape(m, D_MODEL), oa.reshape(m, VA_W), ob.reshape(m, VB_W), g, dn, gn, wo, fn,
                  wfg, wfu, wfd, _pick_tile(m, 256))
        return y.reshape(x3d.shape)

    q16, k32, k16, v32, v16, qb, kb, vb, la, g = front(x_prompt)
    tq = tk = _pick_tile(s, 512)
    slope_rows = jnp.asarray(np.stack([slopes[:, None] * np.arange(tk, dtype=np.float32)[None, :],
                                       np.broadcast_to(slopes[:, None], (N_HEADS_A, tk))], axis=1))
    r3 = lambda a: a.reshape(b, s, a.shape[-1])
    oa = _attn_prompt(lv, slope_rows, r3(q16), r3(k16), r3(v16), tq, tk)
    chunk = math.gcd(s, GLA_CHUNK)
    ob, s_p = _gla(r3(qb), r3(kb), r3(vb), r3(la), jnp.zeros((b, N_HEADS_B, DK_B, DV_B), F32), chunk)
    y_prompt = back(x_prompt, oa, ob, g)
    new_k_prompt = k32.reshape(1, b, s, N_HEADS_A, 2 * HEAD_DIM_A)
    new_v_prompt = v32.reshape(1, b, s, N_HEADS_A, V_DIM_A)

    q16, k32, k16, v32, v16, qb, kb, vb, la, g = front(x_sample)
    r3 = lambda a: a.reshape(db, t_new, a.shape[-1])
    pad16 = lambda a: jnp.pad(r3(a), ((0, 0), (0, 16 - t_new), (0, 0)))
    r4 = lambda a: a.reshape(db, t_new, N_HEADS_A, LANES)
    pad_new = lambda a: jnp.pad(r4(a), ((0, 0), (0, NEW_PAD - t_new), (0, 0), (0, 0)))
    cols = np.arange(COLS_S)
    head_info = np.zeros((3, N_HEADS_A, COLS_S), np.float32)
    head_info[0] = slopes[:, None]
    head_info[1] = np.minimum(cols // 16, t_new - 1)[None, :]
    head_info[2] = ((cols % 8)[None, :] == np.arange(N_HEADS_A)[:, None]) & (cols < 16 * t_new)[None, :]
    ck = cache_k.reshape(depth * n_pool, PAGE_SIZE, N_HEADS_A, 2 * HEAD_DIM_A)
    cv = cache_v.reshape(depth * n_pool, PAGE_SIZE, N_HEADS_A, V_DIM_A)
    gp = 4 if page_table.shape[1] % 4 == 0 else 1
    oa = _attn_sample(page_table, lv, jnp.asarray(head_info), r4(q16.astype(F32)), pad_new(k32), pad_new(v32),
                      ck, cv, gp)
    ob, s_s = _gla(pad16(qb), pad16(kb), pad16(vb), pad16(la), state_gla[0], 16)
    y_sample = back(x_sample, oa, ob[:, :t_new], g)
    new_k_sample = k32.reshape(1, db, t_new, N_HEADS_A, 2 * HEAD_DIM_A)
    new_v_sample = v32.reshape(1, db, t_new, N_HEADS_A, V_DIM_A)

    return (y_prompt, y_sample, new_k_prompt, new_v_prompt, s_p[None], new_k_sample, new_v_sample, s_s[None])
```

```python
import functools
import math

import numpy as np
import jax
import jax.numpy as jnp
from jax import lax
from jax.experimental import pallas as pl
from jax.experimental.pallas import tpu as pltpu

D_MODEL = 1024
N_HEADS_A = 8
HEAD_DIM_A = 64
V_DIM_A = 2 * HEAD_DIM_A
SCALE_A = HEAD_DIM_A ** -0.5
N_HEADS_B = 4
DK_B = 128
DV_B = 256
ALPHA_RANK = 16
GLA_TAU = 16.0
GLA_CHUNK = 64
FFN_HIDDEN = 2816
EPS = 1e-6
PAGE_SIZE = 128
LAM_INIT = 0.8 - 0.6 * math.exp(-0.3 * 0)

QA_W = N_HEADS_A * 2 * HEAD_DIM_A
VA_W = N_HEADS_A * V_DIM_A
QB_W = N_HEADS_B * DK_B
VB_W = N_HEADS_B * DV_B
MAIN_W = 3 * QA_W + 2 * QB_W + VB_W
GATE_W = 2 * D_MODEL

LOG2E = math.log2(math.e)
VT_ROWS = V_DIM_A + 16

LANES = 128
V7X_VMEM_LIMIT_BYTES = 56 * 1024 * 1024

F32 = jnp.float32
BF16 = jnp.bfloat16
NEG_INF = float("-inf")


def _const_spec(shape):
    nd = len(shape)
    return pl.BlockSpec(shape, lambda *_: (0,) * nd, pipeline_mode=pl.Buffered(1))


def _params(semantics):
    return pltpu.CompilerParams(dimension_semantics=semantics, vmem_limit_bytes=V7X_VMEM_LIMIT_BYTES)


def _rms_rows(x, w):
    ms = jnp.mean(x * x, axis=-1, keepdims=True)
    return x * lax.rsqrt(ms + EPS) * w


def _sigmoid(x):
    return 1.0 / (1.0 + jnp.exp(-x))


def _log_sigmoid(x):
    return jnp.minimum(x, 0.0) - jnp.log1p(jnp.exp(-jnp.abs(x)))


def _nt_dot(a, b):
    return lax.dot_general(a, b, (((1,), (1,)), ((), ())), preferred_element_type=F32)


def _map_norm(z, w2):
    lane = lax.broadcasted_iota(jnp.int32, z.shape, 1)
    first = lane < HEAD_DIM_A
    sq = z * z
    s1 = jnp.sum(jnp.where(first, sq, 0.0), axis=-1, keepdims=True)
    s2 = jnp.sum(jnp.where(first, 0.0, sq), axis=-1, keepdims=True)
    ms = jnp.where(first, s1, s2) * (1.0 / HEAD_DIM_A)
    return z * lax.rsqrt(ms + EPS) * w2


def _front_kernel(x_ref, nw_ref, wm_ref, wa_ref, wg_ref, wup_ref, ba_ref, qw_ref, kw_ref,
                  q16_ref, k32_ref, k16_ref, v32_ref, vt16_ref, qb_ref, kb_ref, vb_ref, la_ref, g_ref):
    h = _rms_rows(x_ref[...], nw_ref[...]).astype(BF16)

    def proj(w):
        return jnp.dot(h, w, preferred_element_type=F32)

    zq = proj(wm_ref[:, 0:QA_W])
    for hd in range(N_HEADS_A):
        sl = slice(hd * LANES, (hd + 1) * LANES)
        q16_ref[:, sl] = (_map_norm(zq[:, sl], qw_ref[...]) * (SCALE_A * LOG2E)).astype(BF16)
    zk = proj(wm_ref[:, QA_W:2 * QA_W])
    for hd in range(N_HEADS_A):
        sl = slice(hd * LANES, (hd + 1) * LANES)
        kn = _map_norm(zk[:, sl], kw_ref[...])
        k32_ref[:, sl] = kn
        k16_ref[:, sl] = kn.astype(BF16)
    zv = proj(wm_ref[:, 2 * QA_W:3 * QA_W])
    v32_ref[...] = zv
    tm = zv.shape[0]
    vt = zv.T.reshape(N_HEADS_A, V_DIM_A, tm).astype(BF16)
    extra = lax.broadcasted_iota(jnp.int32, (N_HEADS_A, VT_ROWS - V_DIM_A, tm), 1)
    ones_row = jnp.where(extra == 0, 1.0, 0.0).astype(BF16)
    vt16_ref[0] = jnp.concatenate([vt, ones_row], axis=1)
    o = 3 * QA_W
    qb_ref[...] = proj(wm_ref[:, o:o + QB_W])
    kb_ref[...] = proj(wm_ref[:, o + QB_W:o + 2 * QB_W])
    vb_ref[...] = proj(wm_ref[:, o + 2 * QB_W:o + 2 * QB_W + VB_W])
    a_low = proj(wa_ref[...]).astype(BF16)
    logits = jnp.dot(a_low, wup_ref[...], preferred_element_type=F32) + ba_ref[...]
    la_ref[...] = _log_sigmoid(logits) * (1.0 / GLA_TAU)
    g_ref[...] = _sigmoid(proj(wg_ref[...]))


def _front(x2d, nw, wm, wa, wg, wup, ba, qw2, kw2, tm):
    m = x2d.shape[0]
    row = lambda w: pl.BlockSpec((tm, w), lambda i: (i, 0))
    outs = [(QA_W, BF16), (QA_W, F32), (QA_W, BF16), (VA_W, F32), None,
            (QB_W, F32), (QB_W, F32), (VB_W, F32), (QB_W, F32), (GATE_W, F32)]
    vt_shape = (m // tm, N_HEADS_A, VT_ROWS, tm)
    vt_spec = pl.BlockSpec((1, N_HEADS_A, VT_ROWS, tm), lambda i: (i, 0, 0, 0))
    return pl.pallas_call(
        _front_kernel,
        grid=(m // tm,),
        in_specs=[row(D_MODEL), _const_spec(nw.shape), _const_spec(wm.shape), _const_spec(wa.shape),
                  _const_spec(wg.shape), _const_spec(wup.shape), _const_spec(ba.shape),
                  _const_spec(qw2.shape), _const_spec(kw2.shape)],
        out_specs=[vt_spec if o is None else row(o[0]) for o in outs],
        out_shape=[jax.ShapeDtypeStruct(vt_shape, BF16) if o is None else jax.ShapeDtypeStruct((m, o[0]), o[1])
                   for o in outs],
        compiler_params=_params(("parallel",)),
        name="front",
    )(x2d, nw, wm, wa, wg, wup, ba, qw2, kw2)


def _lambda_from(lv_ref):
    lv = lv_ref[...]
    a = jnp.sum(lv[0:1] * lv[1:2], axis=-1, keepdims=True)
    b = jnp.sum(lv[2:3] * lv[3:4], axis=-1, keepdims=True)
    return jnp.exp(a) - jnp.exp(b) + LAM_INIT


ATTN_COL_BLOCK = 256
LOGITS_AHEAD = 2


def _attn_prompt_kernel(lv_ref, br_ref, sr_ref, q_ref, k_ref, vt_ref, o_ref, qst_ref, m_ref, acc_ref,
                        *, tq, tk, tkv):
    i = pl.program_id(2)
    cb = ATTN_COL_BLOCK
    n_cb = 2 * tq // cb
    n_sub = tk // tkv
    n_full = (i * tq) // tk
    n_diag = tq // tk

    qt = q_ref[0].astype(F32).T
    row = lax.broadcasted_iota(jnp.int32, qt.shape, 0)
    qst_ref[:, 0:tq] = jnp.where(row < HEAD_DIM_A, qt, 0.0).astype(BF16)
    qst_ref[:, tq:2 * tq] = jnp.where(row < HEAD_DIM_A, 0.0, qt).astype(BF16)
    m_ref[...] = jnp.full(m_ref.shape, NEG_INF, F32)
    acc_ref[...] = jnp.zeros(acc_ref.shape, F32)

    def logits(j, c, masked):
        k = k_ref[0, pl.ds(pl.multiple_of(j * tk, tk), tk), :]
        off = j * tk - i * tq
        bias = br_ref[0] + sr_ref[0, 0:1, :] * off.astype(F32)
        bias = jnp.concatenate([bias] * (cb // LANES), axis=1)
        s = jnp.dot(k, qst_ref[:, c * cb:(c + 1) * cb], preferred_element_type=F32) + bias
        if masked:
            kr = lax.broadcasted_iota(jnp.int32, s.shape, 0) + off
            qc = lax.broadcasted_iota(jnp.int32, s.shape, 1) + (c * cb) % tq
            s = jnp.where(kr <= qc, s, NEG_INF)
        return s.reshape(tk // 8, 8, cb)

    def online_tile(j, masked):
        vt = jnp.concatenate([vt_ref[0, j * n_sub + u, 0] for u in range(n_sub)], axis=1)
        pending = [logits(j, c, masked) for c in range(min(LOGITS_AHEAD, n_cb))]
        for c in range(n_cb):
            cols = slice(c * cb, (c + 1) * cb)
            s = pending.pop(0)
            if c + LOGITS_AHEAD < n_cb:
                pending.append(logits(j, c + LOGITS_AHEAD, masked))
            m_old = m_ref[:, cols]
            m_new = jnp.maximum(m_old, jnp.max(jnp.max(s, axis=0), axis=0, keepdims=True))
            alpha = jnp.exp2(m_old - m_new)
            p16 = jnp.exp2(s - m_new).reshape(tk, cb).astype(BF16)
            pv = jnp.dot(vt, p16, preferred_element_type=F32)
            acc = acc_ref[:, cols].reshape(VT_ROWS // 8, 8, cb) * alpha
            acc_ref[:, cols] = acc.reshape(VT_ROWS, cb) + pv
            m_ref[:, cols] = m_new

    def body(j, carry):
        online_tile(j, False)
        return carry

    lax.fori_loop(0, n_full, body, 0)
    for d in range(n_diag):
        online_tile(n_full + d, True)

    lam = _lambda_from(lv_ref)
    l = acc_ref[V_DIM_A:V_DIM_A + 1, :]
    o1 = acc_ref[0:V_DIM_A, 0:tq] / l[:, 0:tq]
    o2 = acc_ref[0:V_DIM_A, tq:2 * tq] / l[:, tq:2 * tq]
    o_ref[0] = (o1 - lam * o2).T


def _attn_prompt(lv, bias_rows, slope_rep, q16, k16, vt16, tq, tk):
    b, s, _ = q16.shape
    n_kv, tkv = vt16.shape[1], vt16.shape[-1]
    assert tq % tk == 0 and tk % tkv == 0 and (2 * tq) % ATTN_COL_BLOCK == 0
    return pl.pallas_call(
        functools.partial(_attn_prompt_kernel, tq=tq, tk=tk, tkv=tkv),
        grid=(b, N_HEADS_A, s // tq),
        in_specs=[pl.BlockSpec(lv.shape, lambda bi, h, i: (0, 0)),
                  pl.BlockSpec((1, tk, LANES), lambda bi, h, i: (h, 0, 0)),
                  pl.BlockSpec((1, 8, LANES), lambda bi, h, i: (h, 0, 0)),
                  pl.BlockSpec((1, tq, LANES), lambda bi, h, i: (bi, i, h)),
                  pl.BlockSpec((1, s, LANES), lambda bi, h, i: (bi, 0, h)),
                  pl.BlockSpec((1, n_kv, 1, VT_ROWS, tkv), lambda bi, h, i: (bi, 0, h, 0, 0))],
        out_specs=pl.BlockSpec((1, tq, LANES), lambda bi, h, i: (bi, i, h)),
        out_shape=jax.ShapeDtypeStruct((b, s, VA_W), F32),
        scratch_shapes=[pltpu.VMEM((2 * HEAD_DIM_A, 2 * tq), BF16),
                        pltpu.VMEM((8, 2 * tq), F32),
                        pltpu.VMEM((VT_ROWS, 2 * tq), F32)],
        compiler_params=_params(("parallel", "parallel", "arbitrary")),
        name="attn_prompt",
    )(lv, bias_rows, slope_rep, q16, k16, vt16)


COLS_S = 128
NEW_PAD = 8


def _attn_sample_kernel(pt_ref, lv_ref, hv_ref, q_ref, kn_ref, vn_ref, *rest, gp, t_new, past_len):
    k_refs = rest[:gp]
    v_refs = rest[gp:2 * gp]
    o_ref, qr_ref, m_ref, l_ref, acc_ref = rest[2 * gp:]
    p = pl.program_id(1)
    slope = hv_ref[0]
    qpos = hv_ref[1] + float(past_len)
    own = hv_ref[2] > 0.5

    @pl.when(p == 0)
    def _():
        lane = lax.broadcasted_iota(jnp.int32, (N_HEADS_A, LANES), 1)
        tiles = []
        for t in range(t_new):
            qt = q_ref[0, t]
            tiles.append(jnp.where(lane < HEAD_DIM_A, qt, 0.0))
            tiles.append(jnp.where(lane < HEAD_DIM_A, 0.0, qt))
        tiles.append(jnp.zeros((COLS_S - 16 * t_new, LANES), F32))
        qr_ref[...] = jnp.concatenate(tiles, axis=0).astype(BF16)
        m_ref[...] = jnp.full(m_ref.shape, NEG_INF, F32)
        l_ref[...] = jnp.zeros(l_ref.shape, F32)
        acc_ref[...] = jnp.zeros(acc_ref.shape, F32)

    def update(k2, v2, kpos, mask):
        n = k2.shape[0] // N_HEADS_A
        s = _nt_dot(k2, qr_ref[...]).reshape(n, N_HEADS_A, COLS_S)
        s = s - slope * (qpos - kpos)
        if mask is not None:
            s = jnp.where(mask, s, NEG_INF)
        m_old = m_ref[...]
        m_new = jnp.maximum(m_old, jnp.max(s, axis=0))
        alpha = jnp.exp2(m_old - m_new)
        pr = jnp.exp2(s - m_new)
        l_ref[...] = alpha * l_ref[...] + jnp.sum(pr, axis=0)
        m_ref[...] = m_new
        pm = jnp.where(own, pr, 0.0).reshape(n * N_HEADS_A, COLS_S).astype(BF16)
        pv = lax.dot_general(v2, pm, (((0,), (0,)), ((), ())), preferred_element_type=F32)
        alpha_row = jnp.sum(jnp.where(own, alpha, 0.0), axis=0, keepdims=True)
        acc_ref[...] = acc_ref[...] * alpha_row + pv

    n_keys = gp * PAGE_SIZE
    rows = lambda r: r[0].reshape(PAGE_SIZE * N_HEADS_A, LANES).astype(BF16)
    k2 = jnp.concatenate([rows(r) for r in k_refs], axis=0)
    v2 = jnp.concatenate([rows(r) for r in v_refs], axis=0)
    kpos = (lax.broadcasted_iota(jnp.int32, (n_keys, N_HEADS_A, COLS_S), 0) + p * n_keys).astype(F32)
    update(k2, v2, kpos, None)

    @pl.when(p == pl.num_programs(1) - 1)
    def _():
        pos = lax.broadcasted_iota(jnp.int32, (NEW_PAD, N_HEADS_A, COLS_S), 0)
        kposn = pos.astype(F32) + float(past_len)
        mask = jnp.logical_and(pos < t_new, kposn <= qpos)
        new_rows = lambda r: r[0].reshape(NEW_PAD * N_HEADS_A, LANES).astype(BF16)
        update(new_rows(kn_ref), new_rows(vn_ref), kposn, mask)
        lam = _lambda_from(lv_ref)
        l_row = jnp.sum(jnp.where(own, l_ref[...], 0.0), axis=0, keepdims=True)
        used = jnp.sum(jnp.where(own, 1.0, 0.0), axis=0, keepdims=True) > 0.5
        o = (acc_ref[...] / jnp.where(used, l_row, 1.0)).T
        for t in range(t_new):
            o_ref[0, t] = o[16 * t:16 * t + 8, :] - lam * o[16 * t + 8:16 * t + 16, :]


def _attn_sample(page_table, lv, head_info, q, kn, vn, cache_k, cache_v, gp):
    db, t_new = q.shape[:2]
    n_pages = page_table.shape[1]
    past_len = n_pages * PAGE_SIZE
    blk4 = lambda n: (1, n, N_HEADS_A, LANES)

    def page_spec(g):
        return pl.BlockSpec(blk4(PAGE_SIZE), lambda b, p, pt: (pt[b, p * gp + g], 0, 0, 0))

    grid_spec = pltpu.PrefetchScalarGridSpec(
        num_scalar_prefetch=1,
        grid=(db, n_pages // gp),
        in_specs=[pl.BlockSpec(lv.shape, lambda b, p, pt: (0, 0)),
                  pl.BlockSpec(head_info.shape, lambda b, p, pt: (0, 0, 0)),
                  pl.BlockSpec(blk4(t_new), lambda b, p, pt: (b, 0, 0, 0)),
                  pl.BlockSpec(blk4(NEW_PAD), lambda b, p, pt: (b, 0, 0, 0)),
                  pl.BlockSpec(blk4(NEW_PAD), lambda b, p, pt: (b, 0, 0, 0))]
                 + [page_spec(g) for g in range(gp)] + [page_spec(g) for g in range(gp)],
        out_specs=pl.BlockSpec(blk4(t_new), lambda b, p, pt: (b, 0, 0, 0)),
        scratch_shapes=[pltpu.VMEM((COLS_S, LANES), BF16),
                        pltpu.VMEM((N_HEADS_A, COLS_S), F32),
                        pltpu.VMEM((N_HEADS_A, COLS_S), F32),
                        pltpu.VMEM((V_DIM_A, COLS_S), F32)],
    )
    return pl.pallas_call(
        functools.partial(_attn_sample_kernel, gp=gp, t_new=t_new, past_len=past_len),
        grid_spec=grid_spec,
        out_shape=jax.ShapeDtypeStruct((db, t_new, N_HEADS_A, V_DIM_A), F32),
        compiler_params=_params(("parallel", "arbitrary")),
        name="attn_sample",
    )(page_table, lv, head_info, q, kn, vn, *([cache_k] * gp), *([cache_v] * gp))


def _split3(x):
    hi = x.astype(BF16)
    r1 = x - hi.astype(F32)
    mid = r1.astype(BF16)
    lo = (r1 - mid.astype(F32)).astype(BF16)
    return hi, mid, lo


def _gla_kernel(q_ref, k_ref, v_ref, la_ref, s0_ref, o_ref, sout_ref, st_ref, *, chunk):
    c = pl.program_id(1)

    @pl.when(c == 0)
    def _():
        for h in range(N_HEADS_B):
            st_ref[h] = s0_ref[0, h].T

    row = lax.broadcasted_iota(jnp.int32, (chunk, chunk), 0)
    col = lax.broadcasted_iota(jnp.int32, (chunk, chunk), 1)
    tri = row >= col
    tri16 = jnp.where(tri, 1.0, 0.0).astype(BF16)

    for h in range(N_HEADS_B):
        ks = slice(h * DK_B, (h + 1) * DK_B)
        vs = slice(h * DV_B, (h + 1) * DV_B)
        q = q_ref[0, :, ks] * (DK_B ** -0.5)
        k = k_ref[0, :, ks]
        v = v_ref[0, :, vs]
        hi, mid, lo = _split3(la_ref[0, :, ks])
        cum = (jnp.dot(tri16, hi, preferred_element_type=F32)
               + jnp.dot(tri16, mid, preferred_element_type=F32)
               + jnp.dot(tri16, lo, preferred_element_type=F32))
        last = cum[chunk - 1:chunk, :]
        qd = (q * jnp.exp(cum)).astype(BF16)
        kd = (k * jnp.exp(-cum)).astype(BF16)
        k2 = k * jnp.exp(last - cum)
        att = jnp.where(tri, _nt_dot(qd, kd), 0.0)
        st = st_ref[h]
        v16 = v.astype(BF16)
        o = jnp.dot(att.astype(BF16), v16, preferred_element_type=F32) + _nt_dot(qd, st.astype(BF16))
        o_ref[0, :, vs] = o
        st_ref[h] = st * jnp.exp(last) + jnp.dot(v.T.astype(BF16), k2.astype(BF16), preferred_element_type=F32)

    @pl.when(c == pl.num_programs(1) - 1)
    def _():
        for h in range(N_HEADS_B):
            sout_ref[0, h] = st_ref[h].T


def _gla(qb, kb, vb, la, s0, chunk):
    b, t, _ = qb.shape
    nc = t // chunk
    tok = lambda w: pl.BlockSpec((1, chunk, w), lambda bi, c: (bi, c, 0))
    st_spec = pl.BlockSpec((1, N_HEADS_B, DK_B, DV_B), lambda bi, c: (bi, 0, 0, 0))
    return pl.pallas_call(
        functools.partial(_gla_kernel, chunk=chunk),
        grid=(b, nc),
        in_specs=[tok(QB_W), tok(QB_W), tok(VB_W), tok(QB_W), st_spec],
        out_specs=[tok(VB_W), st_spec],
        out_shape=[jax.ShapeDtypeStruct((b, t, VB_W), F32),
                   jax.ShapeDtypeStruct((b, N_HEADS_B, DK_B, DV_B), F32)],
        scratch_shapes=[pltpu.VMEM((N_HEADS_B, DV_B, DK_B), F32)],
        compiler_params=_params(("parallel", "arbitrary")),
        name="gla",
    )(qb, kb, vb, la, s0)


def _back_kernel(x_ref, oa_ref, ob_ref, g_ref, dn_ref, gn_ref, wo_ref, fn_ref, wg_ref, wu_ref, wd_ref, y_ref):
    parts = []
    for h in range(N_HEADS_A):
        sl = slice(h * V_DIM_A, (h + 1) * V_DIM_A)
        oa = _rms_rows(oa_ref[:, sl], dn_ref[...]) * (1.0 - LAM_INIT)
        parts.append(g_ref[:, sl] * oa)
    mix_a = jnp.concatenate(parts, axis=-1)
    parts = []
    for h in range(N_HEADS_B):
        sl = slice(h * DV_B, (h + 1) * DV_B)
        ob = _rms_rows(ob_ref[:, sl], gn_ref[...])
        parts.append(g_ref[:, D_MODEL + h * DV_B:D_MODEL + (h + 1) * DV_B] * ob)
    mix = (mix_a + jnp.concatenate(parts, axis=-1)).astype(BF16)
    x1 = x_ref[...] + jnp.dot(mix, wo_ref[...], preferred_element_type=F32)
    h2 = _rms_rows(x1, fn_ref[...]).astype(BF16)
    gt = jnp.dot(h2, wg_ref[...], preferred_element_type=F32)
    up = jnp.dot(h2, wu_ref[...], preferred_element_type=F32)
    u = (gt * _sigmoid(gt) * up).astype(BF16)
    y_ref[...] = x1 + jnp.dot(u, wd_ref[...], preferred_element_type=F32)


def _back(x2d, oa, ob, g, dn, gn, wo, fn, wg, wu, wd, tm):
    m = x2d.shape[0]
    row = lambda w: pl.BlockSpec((tm, w), lambda i: (i, 0))
    return pl.pallas_call(
        _back_kernel,
        grid=(m // tm,),
        in_specs=[row(D_MODEL), row(VA_W), row(VB_W), row(GATE_W)]
                 + [_const_spec(a.shape) for a in (dn, gn, wo, fn, wg, wu, wd)],
        out_specs=row(D_MODEL),
        out_shape=jax.ShapeDtypeStruct((m, D_MODEL), F32),
        compiler_params=_params(("parallel",)),
        name="back",
    )(x2d, oa, ob, g, dn, gn, wo, fn, wg, wu, wd)


def _alibi_slopes(n):
    return np.array([2.0 ** (-8.0 * (i + 1) / n) for i in range(n)], np.float32)


def _pick_tile(n, pref):
    t = min(pref, n)
    assert n % t == 0, (n, t)
    return t


def kernel(x_prompt, x_sample, cache_k, cache_v, state_gla, page_table, attn_norm_w, w_in, w_alpha_up, b_alpha,
           q_norm_w, k_norm_w, lambda_q1, lambda_k1, lambda_q2, lambda_k2, diff_norm_w, gla_norm_w, w_out,
           ffn_norm_w, w_ffn_gate, w_ffn_up, w_ffn_down):
    depth = w_in.shape[0]
    assert depth == 1, "single-layer step"
    b, s, _ = x_prompt.shape
    db, t_new, _ = x_sample.shape
    n_pool = cache_k.shape[1]
    assert 16 * t_new <= COLS_S and t_new <= NEW_PAD

    w_in0 = w_in[0]
    wm = w_in0[:, :MAIN_W].astype(BF16)
    wa = jnp.pad(w_in0[:, MAIN_W:MAIN_W + ALPHA_RANK], ((0, 0), (0, LANES - ALPHA_RANK))).astype(BF16)
    wg_in = w_in0[:, MAIN_W + ALPHA_RANK:].astype(BF16)
    wup = jnp.pad(w_alpha_up[0], ((0, LANES - ALPHA_RANK), (0, 0))).astype(BF16)
    ba = b_alpha[0][None, :]
    nw = attn_norm_w[0][None, :]
    qw2 = jnp.tile(q_norm_w[0], 2)[None, :]
    kw2 = jnp.tile(k_norm_w[0], 2)[None, :]
    lv = jnp.stack([lambda_q1[0], lambda_k1[0], lambda_q2[0], lambda_k2[0]])
    dn = diff_norm_w[0][None, :]
    gn = gla_norm_w[0][None, :]
    fn = ffn_norm_w[0][None, :]
    wo = w_out[0].astype(BF16)
    wfg = w_ffn_gate[0].astype(BF16)
    wfu = w_ffn_up[0].astype(BF16)
    wfd = w_ffn_down[0].astype(BF16)
    slopes2 = (_alibi_slopes(N_HEADS_A).astype(np.float64) * LOG2E).astype(np.float32)

    def front(x3d):
        m = x3d.shape[0] * x3d.shape[1]
        return _front(x3d.reshape(m, D_MODEL), nw, wm, wa, wg_in, wup, ba, qw2, kw2, _pick_tile(m, 256))

    def back(x3d, oa, ob, g):
        m = x3d.shape[0] * x3d.shape[1]
        y = _back(x3d.reshape(m, D_MODEL), oa.reshape(m, VA_W), ob.reshape(m, VB_W), g, dn, gn, wo, fn,
                  wfg, wfu, wfd, _pick_tile(m, 256))
        return y.reshape(x3d.shape)

    q16, k32, k16, v32, vt16, qb, kb, vb, la, g = front(x_prompt)
    tq, tk = _pick_tile(s, 1024), _pick_tile(s, 512)
    bias_rows = jnp.asarray(np.broadcast_to(
        (slopes2[:, None] * np.arange(tk, dtype=np.float32)[None, :])[:, :, None], (N_HEADS_A, tk, LANES)))
    slope_rep = jnp.asarray(np.broadcast_to(slopes2[:, None, None], (N_HEADS_A, 8, LANES)))
    r3 = lambda a: a.reshape(b, s, a.shape[-1])
    vt16 = vt16.reshape(b, vt16.shape[0] // b, N_HEADS_A, VT_ROWS, vt16.shape[-1])
    oa = _attn_prompt(lv, bias_rows, slope_rep, r3(q16), r3(k16), vt16, tq, tk)
    chunk = math.gcd(s, GLA_CHUNK)
    ob, s_p = _gla(r3(qb), r3(kb), r3(vb), r3(la), jnp.zeros((b, N_HEADS_B, DK_B, DV_B), F32), chunk)
    y_prompt = back(x_prompt, oa, ob, g)
    new_k_prompt = k32.reshape(1, b, s, N_HEADS_A, 2 * HEAD_DIM_A)
    new_v_prompt = v32.reshape(1, b, s, N_HEADS_A, V_DIM_A)

    q16, k32, k16, v32, v16, qb, kb, vb, la, g = front(x_sample)
    r3 = lambda a: a.reshape(db, t_new, a.shape[-1])
    pad16 = lambda a: jnp.pad(r3(a), ((0, 0), (0, 16 - t_new), (0, 0)))
    r4 = lambda a: a.reshape(db, t_new, N_HEADS_A, LANES)
    pad_new = lambda a: jnp.pad(r4(a), ((0, 0), (0, NEW_PAD - t_new), (0, 0), (0, 0)))
    cols = np.arange(COLS_S)
    head_info = np.zeros((3, N_HEADS_A, COLS_S), np.float32)
    head_info[0] = slopes2[:, None]
    head_info[1] = np.minimum(cols // 16, t_new - 1)[None, :]
    head_info[2] = ((cols % 8)[None, :] == np.arange(N_HEADS_A)[:, None]) & (cols < 16 * t_new)[None, :]
    ck = cache_k.reshape(depth * n_pool, PAGE_SIZE, N_HEADS_A, 2 * HEAD_DIM_A)
    cv = cache_v.reshape(depth * n_pool, PAGE_SIZE, N_HEADS_A, V_DIM_A)
    gp = 4 if page_table.shape[1] % 4 == 0 else 1
    oa = _attn_sample(page_table, lv, jnp.asarray(head_info), r4(q16.astype(F32)), pad_new(k32), pad_new(v32),
                      ck, cv, gp)
    ob, s_s = _gla(pad16(qb), pad16(kb), pad16(vb), pad16(la), state_gla[0], 16)
    y_sample = back(x_sample, oa, ob[:, :t_new], g)
    new_k_sample = k32.reshape(1, db, t_new, N_HEADS_A, 2 * HEAD_DIM_A)
    new_v_sample = v32.reshape(1, db, t_new, N_HEADS_A, V_DIM_A)

    return (y_prompt, y_sample, new_k_prompt, new_v_prompt, s_p[None], new_k_sample, new_v_sample, s_s[None])
```

```python
import functools
import math

import numpy as np
import jax
import jax.numpy as jnp
from jax import lax
from jax.experimental import pallas as pl
from jax.experimental.pallas import tpu as pltpu

D_MODEL = 1024
N_HEADS_A = 8
HEAD_DIM_A = 64
V_DIM_A = 2 * HEAD_DIM_A
SCALE_A = HEAD_DIM_A ** -0.5
N_HEADS_B = 4
DK_B = 128
DV_B = 256
ALPHA_RANK = 16
GLA_TAU = 16.0
GLA_CHUNK = 64
FFN_HIDDEN = 2816
EPS = 1e-6
PAGE_SIZE = 128
LAM_INIT = 0.8 - 0.6 * math.exp(-0.3 * 0)

QA_W = N_HEADS_A * 2 * HEAD_DIM_A
VA_W = N_HEADS_A * V_DIM_A
QB_W = N_HEADS_B * DK_B
VB_W = N_HEADS_B * DV_B
MAIN_W = 3 * QA_W + 2 * QB_W + VB_W
GATE_W = 2 * D_MODEL

LOG2E = math.log2(math.e)
VT_ROWS = V_DIM_A + 16

LANES = 128
V7X_VMEM_LIMIT_BYTES = 56 * 1024 * 1024

F32 = jnp.float32
BF16 = jnp.bfloat16
NEG_INF = float("-inf")


def _const_spec(shape):
    nd = len(shape)
    return pl.BlockSpec(shape, lambda *_: (0,) * nd, pipeline_mode=pl.Buffered(1))


def _params(semantics):
    return pltpu.CompilerParams(dimension_semantics=semantics, vmem_limit_bytes=V7X_VMEM_LIMIT_BYTES)


def _rms_rows(x, w):
    ms = jnp.mean(x * x, axis=-1, keepdims=True)
    return x * lax.rsqrt(ms + EPS) * w


def _sigmoid(x):
    return 1.0 / (1.0 + jnp.exp(-x))


def _log_sigmoid(x):
    return jnp.minimum(x, 0.0) - jnp.log1p(jnp.exp(-jnp.abs(x)))


def _nt_dot(a, b):
    return lax.dot_general(a, b, (((1,), (1,)), ((), ())), preferred_element_type=F32)


def _map_norm(z, w2):
    lane = lax.broadcasted_iota(jnp.int32, z.shape, 1)
    first = lane < HEAD_DIM_A
    sq = z * z
    s1 = jnp.sum(jnp.where(first, sq, 0.0), axis=-1, keepdims=True)
    s2 = jnp.sum(jnp.where(first, 0.0, sq), axis=-1, keepdims=True)
    ms = jnp.where(first, s1, s2) * (1.0 / HEAD_DIM_A)
    return z * lax.rsqrt(ms + EPS) * w2


def _front_kernel(x_ref, nw_ref, wm_ref, wa_ref, wg_ref, wup_ref, ba_ref, qw_ref, kw_ref,
                  q16_ref, k32_ref, k16_ref, v32_ref, vt16_ref, qb_ref, kb_ref, vb_ref, la_ref, g_ref):
    h = _rms_rows(x_ref[...], nw_ref[...]).astype(BF16)

    def proj(w):
        return jnp.dot(h, w, preferred_element_type=F32)

    zq = proj(wm_ref[:, 0:QA_W])
    for hd in range(N_HEADS_A):
        sl = slice(hd * LANES, (hd + 1) * LANES)
        q16_ref[:, sl] = (_map_norm(zq[:, sl], qw_ref[...]) * (SCALE_A * LOG2E)).astype(BF16)
    zk = proj(wm_ref[:, QA_W:2 * QA_W])
    for hd in range(N_HEADS_A):
        sl = slice(hd * LANES, (hd + 1) * LANES)
        kn = _map_norm(zk[:, sl], kw_ref[...])
        k32_ref[:, sl] = kn
        k16_ref[:, sl] = kn.astype(BF16)
    zv = proj(wm_ref[:, 2 * QA_W:3 * QA_W])
    v32_ref[...] = zv
    tm = zv.shape[0]
    vt = zv.T.reshape(N_HEADS_A, V_DIM_A, tm).astype(BF16)
    extra = lax.broadcasted_iota(jnp.int32, (N_HEADS_A, VT_ROWS - V_DIM_A, tm), 1)
    ones_row = jnp.where(extra == 0, 1.0, 0.0).astype(BF16)
    vt16_ref[0] = jnp.concatenate([vt, ones_row], axis=1)
    o = 3 * QA_W
    qb_ref[...] = proj(wm_ref[:, o:o + QB_W])
    kb_ref[...] = proj(wm_ref[:, o + QB_W:o + 2 * QB_W])
    vb_ref[...] = proj(wm_ref[:, o + 2 * QB_W:o + 2 * QB_W + VB_W])
    a_low = proj(wa_ref[...]).astype(BF16)
    logits = jnp.dot(a_low, wup_ref[...], preferred_element_type=F32) + ba_ref[...]
    la_ref[...] = _log_sigmoid(logits) * (1.0 / GLA_TAU)
    g_ref[...] = _sigmoid(proj(wg_ref[...]))


def _front(x2d, nw, wm, wa, wg, wup, ba, qw2, kw2, tm):
    m = x2d.shape[0]
    row = lambda w: pl.BlockSpec((tm, w), lambda i: (i, 0))
    outs = [(QA_W, BF16), (QA_W, F32), (QA_W, BF16), (VA_W, F32), None,
            (QB_W, F32), (QB_W, F32), (VB_W, F32), (QB_W, F32), (GATE_W, F32)]
    vt_shape = (m // tm, N_HEADS_A, VT_ROWS, tm)
    vt_spec = pl.BlockSpec((1, N_HEADS_A, VT_ROWS, tm), lambda i: (i, 0, 0, 0))
    return pl.pallas_call(
        _front_kernel,
        grid=(m // tm,),
        in_specs=[row(D_MODEL), _const_spec(nw.shape), _const_spec(wm.shape), _const_spec(wa.shape),
                  _const_spec(wg.shape), _const_spec(wup.shape), _const_spec(ba.shape),
                  _const_spec(qw2.shape), _const_spec(kw2.shape)],
        out_specs=[vt_spec if o is None else row(o[0]) for o in outs],
        out_shape=[jax.ShapeDtypeStruct(vt_shape, BF16) if o is None else jax.ShapeDtypeStruct((m, o[0]), o[1])
                   for o in outs],
        compiler_params=_params(("parallel",)),
        name="front",
    )(x2d, nw, wm, wa, wg, wup, ba, qw2, kw2)


def _lambda_from(lv_ref):
    lv = lv_ref[...]
    a = jnp.sum(lv[0:1] * lv[1:2], axis=-1, keepdims=True)
    b = jnp.sum(lv[2:3] * lv[3:4], axis=-1, keepdims=True)
    return jnp.exp(a) - jnp.exp(b) + LAM_INIT


ATTN_COL_BLOCK = 512
LOGITS_AHEAD = 2


def _attn_prompt_kernel(lv_ref, br_ref, sr_ref, q_ref, k_ref, vt_ref, o_ref, qst_ref, m_ref, acc_ref,
                        *, tq, tk, tkv):
    i = pl.program_id(2)
    cb = ATTN_COL_BLOCK
    n_cb = 2 * tq // cb
    n_sub = tk // tkv
    n_full = (i * tq) // tk
    n_diag = tq // tk

    qt = q_ref[0].astype(F32).T
    row = lax.broadcasted_iota(jnp.int32, qt.shape, 0)
    qst_ref[:, 0:tq] = jnp.where(row < HEAD_DIM_A, qt, 0.0).astype(BF16)
    qst_ref[:, tq:2 * tq] = jnp.where(row < HEAD_DIM_A, 0.0, qt).astype(BF16)
    m_ref[...] = jnp.full(m_ref.shape, NEG_INF, F32)
    acc_ref[...] = jnp.zeros(acc_ref.shape, F32)

    def logits(j, c, masked):
        k = k_ref[0, pl.ds(pl.multiple_of(j * tk, tk), tk), :]
        off = j * tk - i * tq
        bias = br_ref[0] + sr_ref[0, 0:1, :] * off.astype(F32)
        bias = jnp.concatenate([bias] * (cb // LANES), axis=1)
        s = jnp.dot(k, qst_ref[:, c * cb:(c + 1) * cb], preferred_element_type=F32) + bias
        if masked:
            kr = lax.broadcasted_iota(jnp.int32, s.shape, 0) + off
            qc = lax.broadcasted_iota(jnp.int32, s.shape, 1) + (c * cb) % tq
            s = jnp.where(kr <= qc, s, NEG_INF)
        return s.reshape(tk // 8, 8, cb)

    def online_tile(j, masked):
        vt = jnp.concatenate([vt_ref[0, j * n_sub + u, 0] for u in range(n_sub)], axis=1)
        pending = [logits(j, c, masked) for c in range(min(LOGITS_AHEAD, n_cb))]
        for c in range(n_cb):
            cols = slice(c * cb, (c + 1) * cb)
            s = pending.pop(0)
            if c + LOGITS_AHEAD < n_cb:
                pending.append(logits(j, c + LOGITS_AHEAD, masked))
            m_old = m_ref[:, cols]
            m_new = jnp.maximum(m_old, jnp.max(jnp.max(s, axis=0), axis=0, keepdims=True))
            alpha = jnp.exp2(m_old - m_new)
            p16 = jnp.exp2(s - m_new).reshape(tk, cb).astype(BF16)
            pv = jnp.dot(vt, p16, preferred_element_type=F32)
            acc = acc_ref[:, cols].reshape(VT_ROWS // 8, 8, cb) * alpha
            acc_ref[:, cols] = acc.reshape(VT_ROWS, cb) + pv
            m_ref[:, cols] = m_new

    def body(j, carry):
        online_tile(j, False)
        return carry

    lax.fori_loop(0, n_full, body, 0)
    for d in range(n_diag):
        online_tile(n_full + d, True)

    lam = _lambda_from(lv_ref)
    l = acc_ref[V_DIM_A:V_DIM_A + 1, :]
    o1 = acc_ref[0:V_DIM_A, 0:tq] / l[:, 0:tq]
    o2 = acc_ref[0:V_DIM_A, tq:2 * tq] / l[:, tq:2 * tq]
    o_ref[0] = (o1 - lam * o2).T


def _attn_prompt(lv, bias_rows, slope_rep, q16, k16, vt16, tq, tk):
    b, s, _ = q16.shape
    n_kv, tkv = vt16.shape[1], vt16.shape[-1]
    assert tq % tk == 0 and tk % tkv == 0 and (2 * tq) % ATTN_COL_BLOCK == 0
    return pl.pallas_call(
        functools.partial(_attn_prompt_kernel, tq=tq, tk=tk, tkv=tkv),
        grid=(b, N_HEADS_A, s // tq),
        in_specs=[pl.BlockSpec(lv.shape, lambda bi, h, i: (0, 0)),
                  pl.BlockSpec((1, tk, LANES), lambda bi, h, i: (h, 0, 0)),
                  pl.BlockSpec((1, 8, LANES), lambda bi, h, i: (h, 0, 0)),
                  pl.BlockSpec((1, tq, LANES), lambda bi, h, i: (bi, i, h)),
                  pl.BlockSpec((1, s, LANES), lambda bi, h, i: (bi, 0, h)),
                  pl.BlockSpec((1, n_kv, 1, VT_ROWS, tkv), lambda bi, h, i: (bi, 0, h, 0, 0))],
        out_specs=pl.BlockSpec((1, tq, LANES), lambda bi, h, i: (bi, i, h)),
        out_shape=jax.ShapeDtypeStruct((b, s, VA_W), F32),
        scratch_shapes=[pltpu.VMEM((2 * HEAD_DIM_A, 2 * tq), BF16),
                        pltpu.VMEM((8, 2 * tq), F32),
                        pltpu.VMEM((VT_ROWS, 2 * tq), F32)],
        compiler_params=_params(("parallel", "parallel", "arbitrary")),
        name="attn_prompt",
    )(lv, bias_rows, slope_rep, q16, k16, vt16)


COLS_S = 128
NEW_PAD = 8
PAGES_AHEAD = 2


def _attn_sample_kernel(pt_ref, lv_ref, hv_ref, rel_ref, q_ref, kn_ref, vn_ref, *rest, gp, t_new, past_len):
    k_refs = rest[:gp]
    v_refs = rest[gp:2 * gp]
    o_ref, qr_ref, m_ref, l_ref, acc_ref = rest[2 * gp:]
    p = pl.program_id(1)
    slope = hv_ref[0]
    qpos = hv_ref[1] + float(past_len)
    own = hv_ref[2] > 0.5

    @pl.when(p == 0)
    def _():
        lane = lax.broadcasted_iota(jnp.int32, (N_HEADS_A, LANES), 1)
        tiles = []
        for t in range(t_new):
            qt = q_ref[0, t]
            tiles.append(jnp.where(lane < HEAD_DIM_A, qt, 0.0))
            tiles.append(jnp.where(lane < HEAD_DIM_A, 0.0, qt))
        tiles.append(jnp.zeros((COLS_S - 16 * t_new, LANES), F32))
        qr = jnp.concatenate(tiles, axis=0)
        zeros = jnp.zeros_like(qr)
        qr_ref[...] = jnp.concatenate([jnp.concatenate([qr, zeros], axis=1),
                                       jnp.concatenate([zeros, qr], axis=1)], axis=0).astype(BF16)
        m_ref[...] = jnp.full(m_ref.shape, NEG_INF, F32)
        l_ref[...] = jnp.zeros(l_ref.shape, F32)
        acc_ref[...] = jnp.zeros(acc_ref.shape, F32)

    def key_logits(k_ref_, n):
        k2 = k_ref_[0].reshape(n * N_HEADS_A, LANES).astype(BF16)
        half = k2.shape[0] // 2
        pair = _nt_dot(jnp.concatenate([k2[:half], k2[half:]], axis=1), qr_ref[...])
        return jnp.concatenate([pair[:, :COLS_S], pair[:, COLS_S:]], axis=0).reshape(n, N_HEADS_A, COLS_S)

    def update(t, v_ref_, rel, base, mask):
        n = t.shape[0]
        t = t + rel
        if mask is not None:
            t = jnp.where(mask, t, NEG_INF)
        cc = slope * (qpos - base)
        m_old = m_ref[...]
        m_new = jnp.maximum(m_old, jnp.max(t, axis=0) - cc)
        alpha = jnp.exp2(m_old - m_new)
        pr = jnp.exp2(t - jnp.where(own, m_new + cc, jnp.inf))
        l_ref[...] = alpha * l_ref[...] + jnp.sum(pr, axis=0)
        m_ref[...] = m_new
        pm = pr.reshape(n * N_HEADS_A, COLS_S).astype(BF16)
        v2 = v_ref_[0].reshape(n * N_HEADS_A, LANES).astype(BF16)
        pv = lax.dot_general(v2, pm, (((0,), (0,)), ((), ())), preferred_element_type=F32)
        alpha_row = jnp.sum(jnp.where(own, alpha, 0.0), axis=0, keepdims=True)
        acc_ref[...] = acc_ref[...] * alpha_row + pv

    pending = [key_logits(k_refs[g], PAGE_SIZE) for g in range(min(PAGES_AHEAD, gp))]
    for g in range(gp):
        t = pending.pop(0)
        if g + PAGES_AHEAD < gp:
            pending.append(key_logits(k_refs[g + PAGES_AHEAD], PAGE_SIZE))
        update(t, v_refs[g], rel_ref[...], ((p * gp + g) * PAGE_SIZE).astype(F32), None)

    @pl.when(p == pl.num_programs(1) - 1)
    def _():
        pos = lax.broadcasted_iota(jnp.int32, (NEW_PAD, N_HEADS_A, COLS_S), 0)
        mask = jnp.logical_and(pos < t_new, pos.astype(F32) + float(past_len) <= qpos)
        update(key_logits(kn_ref, NEW_PAD), vn_ref, rel_ref[0:NEW_PAD], float(past_len), mask)
        lam = _lambda_from(lv_ref)
        l_row = jnp.sum(jnp.where(own, l_ref[...], 0.0), axis=0, keepdims=True)
        used = jnp.sum(jnp.where(own, 1.0, 0.0), axis=0, keepdims=True) > 0.5
        o = (acc_ref[...] / jnp.where(used, l_row, 1.0)).T
        for t in range(t_new):
            o_ref[0, t] = o[16 * t:16 * t + 8, :] - lam * o[16 * t + 8:16 * t + 16, :]


def _attn_sample(page_table, lv, head_info, rel, q, kn, vn, cache_k, cache_v, gp):
    db, t_new = q.shape[:2]
    n_pages = page_table.shape[1]
    past_len = n_pages * PAGE_SIZE
    blk4 = lambda n: (1, n, N_HEADS_A, LANES)

    def page_spec(g):
        return pl.BlockSpec(blk4(PAGE_SIZE), lambda b, p, pt: (pt[b, p * gp + g], 0, 0, 0))

    grid_spec = pltpu.PrefetchScalarGridSpec(
        num_scalar_prefetch=1,
        grid=(db, n_pages // gp),
        in_specs=[pl.BlockSpec(lv.shape, lambda b, p, pt: (0, 0)),
                  pl.BlockSpec(head_info.shape, lambda b, p, pt: (0, 0, 0)),
                  pl.BlockSpec(rel.shape, lambda b, p, pt: (0, 0, 0), pipeline_mode=pl.Buffered(1)),
                  pl.BlockSpec(blk4(t_new), lambda b, p, pt: (b, 0, 0, 0)),
                  pl.BlockSpec(blk4(NEW_PAD), lambda b, p, pt: (b, 0, 0, 0)),
                  pl.BlockSpec(blk4(NEW_PAD), lambda b, p, pt: (b, 0, 0, 0))]
                 + [page_spec(g) for g in range(gp)] + [page_spec(g) for g in range(gp)],
        out_specs=pl.BlockSpec(blk4(t_new), lambda b, p, pt: (b, 0, 0, 0)),
        scratch_shapes=[pltpu.VMEM((2 * COLS_S, 2 * LANES), BF16),
                        pltpu.VMEM((N_HEADS_A, COLS_S), F32),
                        pltpu.VMEM((N_HEADS_A, COLS_S), F32),
                        pltpu.VMEM((V_DIM_A, COLS_S), F32)],
    )
    return pl.pallas_call(
        functools.partial(_attn_sample_kernel, gp=gp, t_new=t_new, past_len=past_len),
        grid_spec=grid_spec,
        out_shape=jax.ShapeDtypeStruct((db, t_new, N_HEADS_A, V_DIM_A), F32),
        compiler_params=_params(("parallel", "arbitrary")),
        name="attn_sample",
    )(page_table, lv, head_info, rel, q, kn, vn, *([cache_k] * gp), *([cache_v] * gp))


def _split3(x):
    hi = x.astype(BF16)
    r1 = x - hi.astype(F32)
    mid = r1.astype(BF16)
    lo = (r1 - mid.astype(F32)).astype(BF16)
    return hi, mid, lo


def _gla_kernel(q_ref, k_ref, v_ref, la_ref, s0_ref, o_ref, sout_ref, st_ref, *, chunk):
    c = pl.program_id(1)

    @pl.when(c == 0)
    def _():
        for h in range(N_HEADS_B):
            st_ref[h] = s0_ref[0, h].T

    row = lax.broadcasted_iota(jnp.int32, (chunk, chunk), 0)
    col = lax.broadcasted_iota(jnp.int32, (chunk, chunk), 1)
    tri = row >= col
    tri16 = jnp.where(tri, 1.0, 0.0).astype(BF16)

    heads = range(N_HEADS_B)
    ks = [slice(h * DK_B, (h + 1) * DK_B) for h in heads]
    vs = [slice(h * DV_B, (h + 1) * DV_B) for h in heads]
    k = [k_ref[0, :, ks[h]] for h in heads]
    v = [v_ref[0, :, vs[h]] for h in heads]
    parts = [_split3(la_ref[0, :, ks[h]]) for h in heads]
    cum = [sum(jnp.dot(tri16, part, preferred_element_type=F32) for part in parts[h]) for h in heads]
    last = [cum[h][chunk - 1:chunk, :] for h in heads]
    qd = [(q_ref[0, :, ks[h]] * (DK_B ** -0.5) * jnp.exp(cum[h])).astype(BF16) for h in heads]
    kd = [(k[h] * jnp.exp(-cum[h])).astype(BF16) for h in heads]
    att = [jnp.where(tri, _nt_dot(qd[h], kd[h]), 0.0).astype(BF16) for h in heads]
    st = [st_ref[h] for h in heads]
    carried = [_nt_dot(qd[h], st[h].astype(BF16)) for h in heads]
    for h in heads:
        o_ref[0, :, vs[h]] = jnp.dot(att[h], v[h].astype(BF16), preferred_element_type=F32) + carried[h]
    for h in heads:
        k2 = (k[h] * jnp.exp(last[h] - cum[h])).astype(BF16)
        st_ref[h] = st[h] * jnp.exp(last[h]) + jnp.dot(v[h].T.astype(BF16), k2, preferred_element_type=F32)

    @pl.when(c == pl.num_programs(1) - 1)
    def _():
        for h in range(N_HEADS_B):
            sout_ref[0, h] = st_ref[h].T


def _gla(qb, kb, vb, la, s0, chunk):
    b, t, _ = qb.shape
    nc = t // chunk
    tok = lambda w: pl.BlockSpec((1, chunk, w), lambda bi, c: (bi, c, 0))
    st_spec = pl.BlockSpec((1, N_HEADS_B, DK_B, DV_B), lambda bi, c: (bi, 0, 0, 0))
    return pl.pallas_call(
        functools.partial(_gla_kernel, chunk=chunk),
        grid=(b, nc),
        in_specs=[tok(QB_W), tok(QB_W), tok(VB_W), tok(QB_W), st_spec],
        out_specs=[tok(VB_W), st_spec],
        out_shape=[jax.ShapeDtypeStruct((b, t, VB_W), F32),
                   jax.ShapeDtypeStruct((b, N_HEADS_B, DK_B, DV_B), F32)],
        scratch_shapes=[pltpu.VMEM((N_HEADS_B, DV_B, DK_B), F32)],
        compiler_params=_params(("parallel", "arbitrary")),
        name="gla",
    )(qb, kb, vb, la, s0)


def _back_kernel(x_ref, oa_ref, ob_ref, g_ref, dn_ref, gn_ref, wo_ref, fn_ref, wg_ref, wu_ref, wd_ref, y_ref):
    parts = []
    for h in range(N_HEADS_A):
        sl = slice(h * V_DIM_A, (h + 1) * V_DIM_A)
        oa = _rms_rows(oa_ref[:, sl], dn_ref[...]) * (1.0 - LAM_INIT)
        parts.append(g_ref[:, sl] * oa)
    mix_a = jnp.concatenate(parts, axis=-1)
    parts = []
    for h in range(N_HEADS_B):
        sl = slice(h * DV_B, (h + 1) * DV_B)
        ob = _rms_rows(ob_ref[:, sl], gn_ref[...])
        parts.append(g_ref[:, D_MODEL + h * DV_B:D_MODEL + (h + 1) * DV_B] * ob)
    mix = (mix_a + jnp.concatenate(parts, axis=-1)).astype(BF16)
    x1 = x_ref[...] + jnp.dot(mix, wo_ref[...], preferred_element_type=F32)
    h2 = _rms_rows(x1, fn_ref[...]).astype(BF16)
    gt = jnp.dot(h2, wg_ref[...], preferred_element_type=F32)
    up = jnp.dot(h2, wu_ref[...], preferred_element_type=F32)
    u = (gt * _sigmoid(gt) * up).astype(BF16)
    y_ref[...] = x1 + jnp.dot(u, wd_ref[...], preferred_element_type=F32)


def _back(x2d, oa, ob, g, dn, gn, wo, fn, wg, wu, wd, tm):
    m = x2d.shape[0]
    row = lambda w: pl.BlockSpec((tm, w), lambda i: (i, 0))
    return pl.pallas_call(
        _back_kernel,
        grid=(m // tm,),
        in_specs=[row(D_MODEL), row(VA_W), row(VB_W), row(GATE_W)]
                 + [_const_spec(a.shape) for a in (dn, gn, wo, fn, wg, wu, wd)],
        out_specs=row(D_MODEL),
        out_shape=jax.ShapeDtypeStruct((m, D_MODEL), F32),
        compiler_params=_params(("parallel",)),
        name="back",
    )(x2d, oa, ob, g, dn, gn, wo, fn, wg, wu, wd)


def _alibi_slopes(n):
    return np.array([2.0 ** (-8.0 * (i + 1) / n) for i in range(n)], np.float32)


def _pick_tile(n, pref):
    t = min(pref, n)
    assert n % t == 0, (n, t)
    return t


def kernel(x_prompt, x_sample, cache_k, cache_v, state_gla, page_table, attn_norm_w, w_in, w_alpha_up, b_alpha,
           q_norm_w, k_norm_w, lambda_q1, lambda_k1, lambda_q2, lambda_k2, diff_norm_w, gla_norm_w, w_out,
           ffn_norm_w, w_ffn_gate, w_ffn_up, w_ffn_down):
    depth = w_in.shape[0]
    assert depth == 1, "single-layer step"
    b, s, _ = x_prompt.shape
    db, t_new, _ = x_sample.shape
    n_pool = cache_k.shape[1]
    assert 16 * t_new <= COLS_S and t_new <= NEW_PAD

    w_in0 = w_in[0]
    wm = w_in0[:, :MAIN_W].astype(BF16)
    wa = jnp.pad(w_in0[:, MAIN_W:MAIN_W + ALPHA_RANK], ((0, 0), (0, LANES - ALPHA_RANK))).astype(BF16)
    wg_in = w_in0[:, MAIN_W + ALPHA_RANK:].astype(BF16)
    wup = jnp.pad(w_alpha_up[0], ((0, LANES - ALPHA_RANK), (0, 0))).astype(BF16)
    ba = b_alpha[0][None, :]
    nw = attn_norm_w[0][None, :]
    qw2 = jnp.tile(q_norm_w[0], 2)[None, :]
    kw2 = jnp.tile(k_norm_w[0], 2)[None, :]
    lv = jnp.stack([lambda_q1[0], lambda_k1[0], lambda_q2[0], lambda_k2[0]])
    dn = diff_norm_w[0][None, :]
    gn = gla_norm_w[0][None, :]
    fn = ffn_norm_w[0][None, :]
    wo = w_out[0].astype(BF16)
    wfg = w_ffn_gate[0].astype(BF16)
    wfu = w_ffn_up[0].astype(BF16)
    wfd = w_ffn_down[0].astype(BF16)
    slopes2 = (_alibi_slopes(N_HEADS_A).astype(np.float64) * LOG2E).astype(np.float32)

    def front(x3d):
        m = x3d.shape[0] * x3d.shape[1]
        return _front(x3d.reshape(m, D_MODEL), nw, wm, wa, wg_in, wup, ba, qw2, kw2, _pick_tile(m, 256))

    def back(x3d, oa, ob, g):
        m = x3d.shape[0] * x3d.shape[1]
        y = _back(x3d.reshape(m, D_MODEL), oa.reshape(m, VA_W), ob.reshape(m, VB_W), g, dn, gn, wo, fn,
                  wfg, wfu, wfd, _pick_tile(m, 256))
        return y.reshape(x3d.shape)

    q16, k32, k16, v32, vt16, qb, kb, vb, la, g = front(x_prompt)
    tq, tk = _pick_tile(s, 1024), _pick_tile(s, 1024)
    bias_rows = jnp.asarray(np.broadcast_to(
        (slopes2[:, None] * np.arange(tk, dtype=np.float32)[None, :])[:, :, None], (N_HEADS_A, tk, LANES)))
    slope_rep = jnp.asarray(np.broadcast_to(slopes2[:, None, None], (N_HEADS_A, 8, LANES)))
    r3 = lambda a: a.reshape(b, s, a.shape[-1])
    vt16 = vt16.reshape(b, vt16.shape[0] // b, N_HEADS_A, VT_ROWS, vt16.shape[-1])
    oa = _attn_prompt(lv, bias_rows, slope_rep, r3(q16), r3(k16), vt16, tq, tk)
    chunk = math.gcd(s, GLA_CHUNK)
    ob, s_p = _gla(r3(qb), r3(kb), r3(vb), r3(la), jnp.zeros((b, N_HEADS_B, DK_B, DV_B), F32), chunk)
    y_prompt = back(x_prompt, oa, ob, g)
    new_k_prompt = k32.reshape(1, b, s, N_HEADS_A, 2 * HEAD_DIM_A)
    new_v_prompt = v32.reshape(1, b, s, N_HEADS_A, V_DIM_A)

    q16, k32, k16, v32, v16, qb, kb, vb, la, g = front(x_sample)
    r3 = lambda a: a.reshape(db, t_new, a.shape[-1])
    pad16 = lambda a: jnp.pad(r3(a), ((0, 0), (0, 16 - t_new), (0, 0)))
    r4 = lambda a: a.reshape(db, t_new, N_HEADS_A, LANES)
    pad_new = lambda a: jnp.pad(r4(a), ((0, 0), (0, NEW_PAD - t_new), (0, 0), (0, 0)))
    cols = np.arange(COLS_S)
    head_info = np.zeros((3, N_HEADS_A, COLS_S), np.float32)
    head_info[0] = slopes2[:, None]
    head_info[1] = np.minimum(cols // 16, t_new - 1)[None, :]
    head_info[2] = ((cols % 8)[None, :] == np.arange(N_HEADS_A)[:, None]) & (cols < 16 * t_new)[None, :]
    ck = cache_k.reshape(depth * n_pool, PAGE_SIZE, N_HEADS_A, 2 * HEAD_DIM_A)
    cv = cache_v.reshape(depth * n_pool, PAGE_SIZE, N_HEADS_A, V_DIM_A)
    gp = math.gcd(page_table.shape[1], 8)
    rel = np.broadcast_to((slopes2[None, :] * np.arange(PAGE_SIZE, dtype=np.float32)[:, None])[:, :, None],
                          (PAGE_SIZE, N_HEADS_A, COLS_S))
    oa = _attn_sample(page_table, lv, jnp.asarray(head_info), jnp.asarray(rel), r4(q16.astype(F32)),
                      pad_new(k32), pad_new(v32), ck, cv, gp)
    ob, s_s = _gla(pad16(qb), pad16(kb), pad16(vb), pad16(la), state_gla[0], 16)
    y_sample = back(x_sample, oa, ob[:, :t_new], g)
    new_k_sample = k32.reshape(1, db, t_new, N_HEADS_A, 2 * HEAD_DIM_A)
    new_v_sample = v32.reshape(1, db, t_new, N_HEADS_A, V_DIM_A)

    return (y_prompt, y_sample, new_k_prompt, new_v_prompt, s_p[None], new_k_sample, new_v_sample, s_s[None])
```

```python
import functools
import math

import numpy as np
import jax
import jax.numpy as jnp
from jax import lax
from jax.experimental import pallas as pl
from jax.experimental.pallas import tpu as pltpu

D_MODEL = 1024
N_HEADS_A = 8
HEAD_DIM_A = 64
V_DIM_A = 2 * HEAD_DIM_A
SCALE_A = HEAD_DIM_A ** -0.5
N_HEADS_B = 4
DK_B = 128
DV_B = 256
ALPHA_RANK = 16
GLA_TAU = 16.0
GLA_CHUNK = 64
FFN_HIDDEN = 2816
EPS = 1e-6
PAGE_SIZE = 128
LAM_INIT = 0.8 - 0.6 * math.exp(-0.3 * 0)

QA_W = N_HEADS_A * 2 * HEAD_DIM_A
VA_W = N_HEADS_A * V_DIM_A
QB_W = N_HEADS_B * DK_B
VB_W = N_HEADS_B * DV_B
MAIN_W = 3 * QA_W + 2 * QB_W + VB_W
GATE_W = 2 * D_MODEL

LOG2E = math.log2(math.e)
VT_ROWS = V_DIM_A + 16

LANES = 128
V7X_VMEM_LIMIT_BYTES = 56 * 1024 * 1024

F32 = jnp.float32
BF16 = jnp.bfloat16
NEG_INF = float("-inf")


def _const_spec(shape):
    nd = len(shape)
    return pl.BlockSpec(shape, lambda *_: (0,) * nd, pipeline_mode=pl.Buffered(1))


def _params(semantics):
    return pltpu.CompilerParams(dimension_semantics=semantics, vmem_limit_bytes=V7X_VMEM_LIMIT_BYTES)


def _rms_rows(x, w):
    ms = jnp.mean(x * x, axis=-1, keepdims=True)
    return x * lax.rsqrt(ms + EPS) * w


def _sigmoid(x):
    return 1.0 / (1.0 + jnp.exp(-x))


def _log_sigmoid(x):
    return jnp.minimum(x, 0.0) - jnp.log1p(jnp.exp(-jnp.abs(x)))


def _nt_dot(a, b):
    return lax.dot_general(a, b, (((1,), (1,)), ((), ())), preferred_element_type=F32)


def _map_norm(z, w2):
    lane = lax.broadcasted_iota(jnp.int32, z.shape, 1)
    first = lane < HEAD_DIM_A
    sq = z * z
    s1 = jnp.sum(jnp.where(first, sq, 0.0), axis=-1, keepdims=True)
    s2 = jnp.sum(jnp.where(first, 0.0, sq), axis=-1, keepdims=True)
    ms = jnp.where(first, s1, s2) * (1.0 / HEAD_DIM_A)
    return z * lax.rsqrt(ms + EPS) * w2


def _front_kernel(x_ref, nw_ref, wm_ref, wa_ref, wg_ref, wup_ref, ba_ref, qw_ref, kw_ref,
                  q16_ref, k32_ref, k16_ref, v32_ref, vt16_ref, qb_ref, kb_ref, vb_ref, la_ref, g_ref):
    h = _rms_rows(x_ref[...], nw_ref[...]).astype(BF16)

    def proj(w):
        return jnp.dot(h, w, preferred_element_type=F32)

    zq = proj(wm_ref[:, 0:QA_W])
    for hd in range(N_HEADS_A):
        sl = slice(hd * LANES, (hd + 1) * LANES)
        q16_ref[:, sl] = (_map_norm(zq[:, sl], qw_ref[...]) * (SCALE_A * LOG2E)).astype(BF16)
    zk = proj(wm_ref[:, QA_W:2 * QA_W])
    for hd in range(N_HEADS_A):
        sl = slice(hd * LANES, (hd + 1) * LANES)
        kn = _map_norm(zk[:, sl], kw_ref[...])
        k32_ref[:, sl] = kn
        k16_ref[:, sl] = kn.astype(BF16)
    zv = proj(wm_ref[:, 2 * QA_W:3 * QA_W])
    v32_ref[...] = zv
    tm = zv.shape[0]
    vt = zv.T.reshape(N_HEADS_A, V_DIM_A, tm).astype(BF16)
    extra = lax.broadcasted_iota(jnp.int32, (N_HEADS_A, VT_ROWS - V_DIM_A, tm), 1)
    ones_row = jnp.where(extra == 0, 1.0, 0.0).astype(BF16)
    vt16_ref[0] = jnp.concatenate([vt, ones_row], axis=1)
    o = 3 * QA_W
    qb_ref[...] = proj(wm_ref[:, o:o + QB_W])
    kb_ref[...] = proj(wm_ref[:, o + QB_W:o + 2 * QB_W])
    vb_ref[...] = proj(wm_ref[:, o + 2 * QB_W:o + 2 * QB_W + VB_W])
    a_low = proj(wa_ref[...]).astype(BF16)
    logits = jnp.dot(a_low, wup_ref[...], preferred_element_type=F32) + ba_ref[...]
    la_ref[...] = _log_sigmoid(logits) * (1.0 / GLA_TAU)
    g_ref[...] = _sigmoid(proj(wg_ref[...]))


def _front(x2d, nw, wm, wa, wg, wup, ba, qw2, kw2, tm):
    m = x2d.shape[0]
    row = lambda w: pl.BlockSpec((tm, w), lambda i: (i, 0))
    outs = [(QA_W, BF16), (QA_W, F32), (QA_W, BF16), (VA_W, F32), None,
            (QB_W, F32), (QB_W, F32), (VB_W, F32), (QB_W, F32), (GATE_W, F32)]
    vt_shape = (m // tm, N_HEADS_A, VT_ROWS, tm)
    vt_spec = pl.BlockSpec((1, N_HEADS_A, VT_ROWS, tm), lambda i: (i, 0, 0, 0))
    return pl.pallas_call(
        _front_kernel,
        grid=(m // tm,),
        in_specs=[row(D_MODEL), _const_spec(nw.shape), _const_spec(wm.shape), _const_spec(wa.shape),
                  _const_spec(wg.shape), _const_spec(wup.shape), _const_spec(ba.shape),
                  _const_spec(qw2.shape), _const_spec(kw2.shape)],
        out_specs=[vt_spec if o is None else row(o[0]) for o in outs],
        out_shape=[jax.ShapeDtypeStruct(vt_shape, BF16) if o is None else jax.ShapeDtypeStruct((m, o[0]), o[1])
                   for o in outs],
        compiler_params=_params(("parallel",)),
        name="front",
    )(x2d, nw, wm, wa, wg, wup, ba, qw2, kw2)


def _lambda_from(lv_ref):
    lv = lv_ref[...]
    a = jnp.sum(lv[0:1] * lv[1:2], axis=-1, keepdims=True)
    b = jnp.sum(lv[2:3] * lv[3:4], axis=-1, keepdims=True)
    return jnp.exp(a) - jnp.exp(b) + LAM_INIT


ATTN_COL_BLOCK = 512
LOGITS_AHEAD = 2


def _attn_prompt_kernel(lv_ref, kf_ref, gf_ref, sr_ref, q_ref, k_ref, vt_ref, o_ref, qst_ref, m_ref, acc_ref,
                        pend_ref, *, tq, tk, tkv):
    i = pl.program_id(2)
    cb = ATTN_COL_BLOCK
    n_cb = 2 * tq // cb
    n_sub = tk // tkv
    n_full = (i * tq) // tk
    n_diag = tq // tk

    qt = q_ref[0].astype(F32).T
    row = lax.broadcasted_iota(jnp.int32, qt.shape, 0)
    qst_ref[0:LANES, 0:tq] = jnp.where(row < HEAD_DIM_A, qt, 0.0).astype(BF16)
    qst_ref[0:LANES, tq:2 * tq] = jnp.where(row < HEAD_DIM_A, 0.0, qt).astype(BF16)
    qst_ref[LANES:2 * LANES, :] = jnp.concatenate([gf_ref[0]] * (2 * tq // LANES), axis=1)
    m_ref[...] = jnp.full(m_ref.shape, NEG_INF, F32)
    acc_ref[...] = jnp.zeros(acc_ref.shape, F32)

    def logits(j, c):
        k = k_ref[0, pl.ds(pl.multiple_of(j * tk, tk), tk), :]
        k_aug = jnp.concatenate([k, kf_ref[...]], axis=1)
        s = jnp.dot(k_aug, qst_ref[:, c * cb:(c + 1) * cb], preferred_element_type=F32)
        return s.reshape(tk // 8, 8, cb)

    def online_tile(j, masked, next_j):
        vt = jnp.concatenate([vt_ref[0, j * n_sub + u, 0] for u in range(n_sub)], axis=1)
        tile_bias = jnp.concatenate([sr_ref[0] * (j * tk - i * tq).astype(F32)] * (cb // LANES), axis=1)
        pending = []
        for c in range(n_cb):
            cols = slice(c * cb, (c + 1) * cb)
            s = pend_ref[c] if c < LOGITS_AHEAD else pending.pop(0)
            ahead = c + LOGITS_AHEAD
            if ahead < n_cb:
                pending.append(logits(j, ahead))
            elif next_j is not None:
                pend_ref[ahead - n_cb] = logits(next_j, ahead - n_cb)
            if masked:
                kr = lax.broadcasted_iota(jnp.int32, s.shape, 0) * 8 + lax.broadcasted_iota(jnp.int32, s.shape, 1)
                qc = lax.broadcasted_iota(jnp.int32, s.shape, 2) + (c * cb) % tq
                s = jnp.where(kr + (j * tk - i * tq) <= qc, s, NEG_INF)
            m_old = m_ref[:, cols]
            m_new = jnp.maximum(m_old, jnp.max(jnp.max(s, axis=0), axis=0, keepdims=True) + tile_bias)
            alpha = jnp.exp2(m_old - m_new)
            p16 = jnp.exp2(s - (m_new - tile_bias)).reshape(tk, cb).astype(BF16)
            pv = jnp.dot(vt, p16, preferred_element_type=F32)
            acc = acc_ref[:, cols].reshape(VT_ROWS // 8, 8, cb) * alpha
            acc_ref[:, cols] = acc.reshape(VT_ROWS, cb) + pv
            m_ref[:, cols] = m_new

    for c in range(LOGITS_AHEAD):
        pend_ref[c] = logits(0, c)

    def body(j, carry):
        online_tile(j, False, j + 1)
        return carry

    lax.fori_loop(0, n_full, body, 0)
    for d in range(n_diag):
        online_tile(n_full + d, True, n_full + d + 1 if d + 1 < n_diag else None)

    lam = _lambda_from(lv_ref)
    l = acc_ref[V_DIM_A:V_DIM_A + 1, :]
    o1 = acc_ref[0:V_DIM_A, 0:tq] / l[:, 0:tq]
    o2 = acc_ref[0:V_DIM_A, tq:2 * tq] / l[:, tq:2 * tq]
    o_ref[0] = (o1 - lam * o2).T


def _attn_prompt(lv, key_feat, slope_feat, slope_rep, q16, k16, vt16, tq, tk):
    b, s, _ = q16.shape
    n_kv, tkv = vt16.shape[1], vt16.shape[-1]
    assert tq % tk == 0 and tk % tkv == 0 and (2 * tq) % ATTN_COL_BLOCK == 0
    return pl.pallas_call(
        functools.partial(_attn_prompt_kernel, tq=tq, tk=tk, tkv=tkv),
        grid=(b, N_HEADS_A, s // tq),
        in_specs=[pl.BlockSpec(lv.shape, lambda bi, h, i: (0, 0)),
                  pl.BlockSpec((tk, LANES), lambda bi, h, i: (0, 0)),
                  pl.BlockSpec((1, LANES, LANES), lambda bi, h, i: (h, 0, 0)),
                  pl.BlockSpec((1, 8, LANES), lambda bi, h, i: (h, 0, 0)),
                  pl.BlockSpec((1, tq, LANES), lambda bi, h, i: (bi, i, h)),
                  pl.BlockSpec((1, s, LANES), lambda bi, h, i: (bi, 0, h)),
                  pl.BlockSpec((1, n_kv, 1, VT_ROWS, tkv), lambda bi, h, i: (bi, 0, h, 0, 0))],
        out_specs=pl.BlockSpec((1, tq, LANES), lambda bi, h, i: (bi, i, h)),
        out_shape=jax.ShapeDtypeStruct((b, s, VA_W), F32),
        scratch_shapes=[pltpu.VMEM((2 * LANES, 2 * tq), BF16),
                        pltpu.VMEM((8, 2 * tq), F32),
                        pltpu.VMEM((VT_ROWS, 2 * tq), F32),
                        pltpu.VMEM((LOGITS_AHEAD, tk // 8, 8, ATTN_COL_BLOCK), F32)],
        compiler_params=_params(("parallel", "parallel", "arbitrary")),
        name="attn_prompt",
    )(lv, key_feat, slope_feat, slope_rep, q16, k16, vt16)


COLS_S = 128
NEW_PAD = 8
PAGES_AHEAD = 3


def _attn_sample_kernel(pt_ref, lv_ref, hv_ref, rel_ref, q_ref, kn_ref, vn_ref, *rest, gp, t_new, past_len):
    k_refs = rest[:gp]
    v_refs = rest[gp:2 * gp]
    o_ref, qr_ref, m_ref, l_ref, acc_ref = rest[2 * gp:]
    p = pl.program_id(1)
    slope = hv_ref[0]
    qpos = hv_ref[1] + float(past_len)
    own = hv_ref[2] > 0.5

    @pl.when(p == 0)
    def _():
        lane = lax.broadcasted_iota(jnp.int32, (N_HEADS_A, LANES), 1)
        tiles = []
        for t in range(t_new):
            qt = q_ref[0, t]
            tiles.append(jnp.where(lane < HEAD_DIM_A, qt, 0.0))
            tiles.append(jnp.where(lane < HEAD_DIM_A, 0.0, qt))
        tiles.append(jnp.zeros((COLS_S - 16 * t_new, LANES), F32))
        qr = jnp.concatenate(tiles, axis=0)
        zeros = jnp.zeros_like(qr)
        qr_ref[...] = jnp.concatenate([jnp.concatenate([qr, zeros], axis=1),
                                       jnp.concatenate([zeros, qr], axis=1)], axis=0).astype(BF16)
        m_ref[...] = jnp.full(m_ref.shape, NEG_INF, F32)
        l_ref[...] = jnp.zeros(l_ref.shape, F32)
        acc_ref[...] = jnp.zeros(acc_ref.shape, F32)

    def key_logits(k_ref_, n):
        k2 = k_ref_[0].reshape(n * N_HEADS_A, LANES).astype(BF16)
        half = k2.shape[0] // 2
        pair = _nt_dot(jnp.concatenate([k2[:half], k2[half:]], axis=1), qr_ref[...])
        return jnp.concatenate([pair[:, :COLS_S], pair[:, COLS_S:]], axis=0).reshape(n, N_HEADS_A, COLS_S)

    def update(t, v_ref_, rel, base, mask):
        n = t.shape[0]
        t = t + rel
        if mask is not None:
            t = jnp.where(mask, t, NEG_INF)
        cc = slope * (qpos - base)
        m_old = m_ref[...]
        m_new = jnp.maximum(m_old, jnp.max(t, axis=0) - cc)
        alpha = jnp.exp2(m_old - m_new)
        pr = jnp.exp2(t - jnp.where(own, m_new + cc, jnp.inf))
        l_ref[...] = alpha * l_ref[...] + jnp.sum(pr, axis=0)
        m_ref[...] = m_new
        pm = pr.reshape(n * N_HEADS_A, COLS_S).astype(BF16)
        v2 = v_ref_[0].reshape(n * N_HEADS_A, LANES).astype(BF16)
        pv = lax.dot_general(v2, pm, (((0,), (0,)), ((), ())), preferred_element_type=F32)
        alpha_row = jnp.sum(jnp.where(own, alpha, 0.0), axis=0, keepdims=True)
        acc_ref[...] = acc_ref[...] * alpha_row + pv

    pending = [key_logits(k_refs[g], PAGE_SIZE) for g in range(min(PAGES_AHEAD, gp))]
    for g in range(gp):
        t = pending.pop(0)
        if g + PAGES_AHEAD < gp:
            pending.append(key_logits(k_refs[g + PAGES_AHEAD], PAGE_SIZE))
        update(t, v_refs[g], rel_ref[...], ((p * gp + g) * PAGE_SIZE).astype(F32), None)

    @pl.when(p == pl.num_programs(1) - 1)
    def _():
        pos = lax.broadcasted_iota(jnp.int32, (NEW_PAD, N_HEADS_A, COLS_S), 0)
        mask = jnp.logical_and(pos < t_new, pos.astype(F32) + float(past_len) <= qpos)
        update(key_logits(kn_ref, NEW_PAD), vn_ref, rel_ref[0:NEW_PAD], float(past_len), mask)
        lam = _lambda_from(lv_ref)
        l_row = jnp.sum(jnp.where(own, l_ref[...], 0.0), axis=0, keepdims=True)
        used = jnp.sum(jnp.where(own, 1.0, 0.0), axis=0, keepdims=True) > 0.5
        o = (acc_ref[...] / jnp.where(used, l_row, 1.0)).T
        for t in range(t_new):
            o_ref[0, t] = o[16 * t:16 * t + 8, :] - lam * o[16 * t + 8:16 * t + 16, :]


def _attn_sample(page_table, lv, head_info, rel, q, kn, vn, cache_k, cache_v, gp):
    db, t_new = q.shape[:2]
    n_pages = page_table.shape[1]
    past_len = n_pages * PAGE_SIZE
    blk4 = lambda n: (1, n, N_HEADS_A, LANES)

    def page_spec(g):
        return pl.BlockSpec(blk4(PAGE_SIZE), lambda b, p, pt: (pt[b, p * gp + g], 0, 0, 0))

    grid_spec = pltpu.PrefetchScalarGridSpec(
        num_scalar_prefetch=1,
        grid=(db, n_pages // gp),
        in_specs=[pl.BlockSpec(lv.shape, lambda b, p, pt: (0, 0)),
                  pl.BlockSpec(head_info.shape, lambda b, p, pt: (0, 0, 0)),
                  pl.BlockSpec(rel.shape, lambda b, p, pt: (0, 0, 0), pipeline_mode=pl.Buffered(1)),
                  pl.BlockSpec(blk4(t_new), lambda b, p, pt: (b, 0, 0, 0)),
                  pl.BlockSpec(blk4(NEW_PAD), lambda b, p, pt: (b, 0, 0, 0)),
                  pl.BlockSpec(blk4(NEW_PAD), lambda b, p, pt: (b, 0, 0, 0))]
                 + [page_spec(g) for g in range(gp)] + [page_spec(g) for g in range(gp)],
        out_specs=pl.BlockSpec(blk4(t_new), lambda b, p, pt: (b, 0, 0, 0)),
        scratch_shapes=[pltpu.VMEM((2 * COLS_S, 2 * LANES), BF16),
                        pltpu.VMEM((N_HEADS_A, COLS_S), F32),
                        pltpu.VMEM((N_HEADS_A, COLS_S), F32),
                        pltpu.VMEM((V_DIM_A, COLS_S), F32)],
    )
    return pl.pallas_call(
        functools.partial(_attn_sample_kernel, gp=gp, t_new=t_new, past_len=past_len),
        grid_spec=grid_spec,
        out_shape=jax.ShapeDtypeStruct((db, t_new, N_HEADS_A, V_DIM_A), F32),
        compiler_params=_params(("parallel", "arbitrary")),
        name="attn_sample",
    )(page_table, lv, head_info, rel, q, kn, vn, *([cache_k] * gp), *([cache_v] * gp))


def _split3(x):
    hi = x.astype(BF16)
    r1 = x - hi.astype(F32)
    mid = r1.astype(BF16)
    lo = (r1 - mid.astype(F32)).astype(BF16)
    return hi, mid, lo


def _gla_kernel(q_ref, k_ref, v_ref, la_ref, s0_ref, o_ref, sout_ref, st_ref, *, chunk):
    c = pl.program_id(1)

    @pl.when(c == 0)
    def _():
        for h in range(N_HEADS_B):
            st_ref[h] = s0_ref[0, h].T

    row = lax.broadcasted_iota(jnp.int32, (chunk, chunk), 0)
    col = lax.broadcasted_iota(jnp.int32, (chunk, chunk), 1)
    tri = row >= col
    tri16 = jnp.where(tri, 1.0, 0.0).astype(BF16)

    heads = range(N_HEADS_B)
    ks = [slice(h * DK_B, (h + 1) * DK_B) for h in heads]
    vs = [slice(h * DV_B, (h + 1) * DV_B) for h in heads]
    k = [k_ref[0, :, ks[h]] for h in heads]
    v = [v_ref[0, :, vs[h]] for h in heads]
    parts = [_split3(la_ref[0, :, ks[h]]) for h in heads]
    cum = [sum(jnp.dot(tri16, part, preferred_element_type=F32) for part in parts[h]) for h in heads]
    last = [cum[h][chunk - 1:chunk, :] for h in heads]
    qd = [(q_ref[0, :, ks[h]] * (DK_B ** -0.5) * jnp.exp(cum[h])).astype(BF16) for h in heads]
    kd = [(k[h] * jnp.exp(-cum[h])).astype(BF16) for h in heads]
    att = [jnp.where(tri, _nt_dot(qd[h], kd[h]), 0.0).astype(BF16) for h in heads]
    st = [st_ref[h] for h in heads]
    carried = [_nt_dot(qd[h], st[h].astype(BF16)) for h in heads]
    for h in heads:
        o_ref[0, :, vs[h]] = jnp.dot(att[h], v[h].astype(BF16), preferred_element_type=F32) + carried[h]
    for h in heads:
        k2 = (k[h] * jnp.exp(last[h] - cum[h])).astype(BF16)
        st_ref[h] = st[h] * jnp.exp(last[h]) + jnp.dot(v[h].T.astype(BF16), k2, preferred_element_type=F32)

    @pl.when(c == pl.num_programs(1) - 1)
    def _():
        for h in range(N_HEADS_B):
            sout_ref[0, h] = st_ref[h].T


def _gla(qb, kb, vb, la, s0, chunk):
    b, t, _ = qb.shape
    nc = t // chunk
    tok = lambda w: pl.BlockSpec((1, chunk, w), lambda bi, c: (bi, c, 0))
    st_spec = pl.BlockSpec((1, N_HEADS_B, DK_B, DV_B), lambda bi, c: (bi, 0, 0, 0))
    return pl.pallas_call(
        functools.partial(_gla_kernel, chunk=chunk),
        grid=(b, nc),
        in_specs=[tok(QB_W), tok(QB_W), tok(VB_W), tok(QB_W), st_spec],
        out_specs=[tok(VB_W), st_spec],
        out_shape=[jax.ShapeDtypeStruct((b, t, VB_W), F32),
                   jax.ShapeDtypeStruct((b, N_HEADS_B, DK_B, DV_B), F32)],
        scratch_shapes=[pltpu.VMEM((N_HEADS_B, DV_B, DK_B), F32)],
        compiler_params=_params(("parallel", "arbitrary")),
        name="gla",
    )(qb, kb, vb, la, s0)


def _back_kernel(x_ref, oa_ref, ob_ref, g_ref, dn_ref, gn_ref, wo_ref, fn_ref, wg_ref, wu_ref, wd_ref, y_ref):
    def gated_mix(r):
        parts = []
        for h in range(N_HEADS_A):
            sl = slice(h * V_DIM_A, (h + 1) * V_DIM_A)
            oa = _rms_rows(oa_ref[r, sl], dn_ref[...]) * (1.0 - LAM_INIT)
            parts.append(g_ref[r, sl] * oa)
        mix_a = jnp.concatenate(parts, axis=-1)
        parts = []
        for h in range(N_HEADS_B):
            sl = slice(h * DV_B, (h + 1) * DV_B)
            ob = _rms_rows(ob_ref[r, sl], gn_ref[...])
            parts.append(g_ref[r, D_MODEL + h * DV_B:D_MODEL + (h + 1) * DV_B] * ob)
        return (mix_a + jnp.concatenate(parts, axis=-1)).astype(BF16)

    tm = x_ref.shape[0]
    halves = [slice(0, tm // 2), slice(tm // 2, tm)]
    dot = lambda a, w_ref: jnp.dot(a, w_ref[...], preferred_element_type=F32)
    mix = [gated_mix(r) for r in halves]
    x1 = [x_ref[r, :] + dot(mix[n], wo_ref) for n, r in enumerate(halves)]
    h2 = [_rms_rows(x, fn_ref[...]).astype(BF16) for x in x1]
    gate_up = [(dot(h, wg_ref), dot(h, wu_ref)) for h in h2]
    u = [(gt * _sigmoid(gt) * up).astype(BF16) for gt, up in gate_up]
    for n, r in enumerate(halves):
        y_ref[r, :] = x1[n] + dot(u[n], wd_ref)


def _back(x2d, oa, ob, g, dn, gn, wo, fn, wg, wu, wd, tm):
    m = x2d.shape[0]
    row = lambda w: pl.BlockSpec((tm, w), lambda i: (i, 0))
    return pl.pallas_call(
        _back_kernel,
        grid=(m // tm,),
        in_specs=[row(D_MODEL), row(VA_W), row(VB_W), row(GATE_W)]
                 + [_const_spec(a.shape) for a in (dn, gn, wo, fn, wg, wu, wd)],
        out_specs=row(D_MODEL),
        out_shape=jax.ShapeDtypeStruct((m, D_MODEL), F32),
        compiler_params=_params(("parallel",)),
        name="back",
    )(x2d, oa, ob, g, dn, gn, wo, fn, wg, wu, wd)


def _alibi_slopes(n):
    return np.array([2.0 ** (-8.0 * (i + 1) / n) for i in range(n)], np.float32)


def _alibi_features(slopes2, tk):
    assert tk <= 256 * 256
    rows = np.arange(tk)
    kf = np.zeros((tk, LANES), np.float32)
    kf[:, 0:3] = (256 * (rows // 256))[:, None]
    kf[:, 3:6] = (rows % 256)[:, None]
    to_bf16 = lambda x: x.astype(jnp.bfloat16).astype(np.float32)
    s1 = to_bf16(slopes2)
    s2 = to_bf16(slopes2 - s1)
    s3 = to_bf16(slopes2 - s1 - s2)
    sf = np.zeros((len(slopes2), LANES, LANES), np.float32)
    for r, term in enumerate((s1, s2, s3, s1, s2, s3)):
        sf[:, r, :] = term[:, None]
    return jnp.asarray(kf, BF16), jnp.asarray(sf, BF16)


def _pick_tile(n, pref):
    t = min(pref, n)
    assert n % t == 0, (n, t)
    return t


def kernel(x_prompt, x_sample, cache_k, cache_v, state_gla, page_table, attn_norm_w, w_in, w_alpha_up, b_alpha,
           q_norm_w, k_norm_w, lambda_q1, lambda_k1, lambda_q2, lambda_k2, diff_norm_w, gla_norm_w, w_out,
           ffn_norm_w, w_ffn_gate, w_ffn_up, w_ffn_down):
    depth = w_in.shape[0]
    assert depth == 1, "single-layer step"
    b, s, _ = x_prompt.shape
    db, t_new, _ = x_sample.shape
    n_pool = cache_k.shape[1]
    assert 16 * t_new <= COLS_S and t_new <= NEW_PAD

    w_in0 = w_in[0]
    wm = w_in0[:, :MAIN_W].astype(BF16)
    wa = jnp.pad(w_in0[:, MAIN_W:MAIN_W + ALPHA_RANK], ((0, 0), (0, LANES - ALPHA_RANK))).astype(BF16)
    wg_in = w_in0[:, MAIN_W + ALPHA_RANK:].astype(BF16)
    wup = jnp.pad(w_alpha_up[0], ((0, LANES - ALPHA_RANK), (0, 0))).astype(BF16)
    ba = b_alpha[0][None, :]
    nw = attn_norm_w[0][None, :]
    qw2 = jnp.tile(q_norm_w[0], 2)[None, :]
    kw2 = jnp.tile(k_norm_w[0], 2)[None, :]
    lv = jnp.stack([lambda_q1[0], lambda_k1[0], lambda_q2[0], lambda_k2[0]])
    dn = diff_norm_w[0][None, :]
    gn = gla_norm_w[0][None, :]
    fn = ffn_norm_w[0][None, :]
    wo = w_out[0].astype(BF16)
    wfg = w_ffn_gate[0].astype(BF16)
    wfu = w_ffn_up[0].astype(BF16)
    wfd = w_ffn_down[0].astype(BF16)
    slopes2 = (_alibi_slopes(N_HEADS_A).astype(np.float64) * LOG2E).astype(np.float32)

    def front(x3d):
        m = x3d.shape[0] * x3d.shape[1]
        return _front(x3d.reshape(m, D_MODEL), nw, wm, wa, wg_in, wup, ba, qw2, kw2, _pick_tile(m, 256))

    def back(x3d, oa, ob, g):
        m = x3d.shape[0] * x3d.shape[1]
        y = _back(x3d.reshape(m, D_MODEL), oa.reshape(m, VA_W), ob.reshape(m, VB_W), g, dn, gn, wo, fn,
                  wfg, wfu, wfd, _pick_tile(m, 256))
        return y.reshape(x3d.shape)

    q16, k32, k16, v32, vt16, qb, kb, vb, la, g = front(x_prompt)
    tq, tk = _pick_tile(s, 1024), _pick_tile(s, 1024)
    key_feat, slope_feat = _alibi_features(slopes2, tk)
    slope_rep = jnp.asarray(np.broadcast_to(slopes2[:, None, None], (N_HEADS_A, 8, LANES)))
    r3 = lambda a: a.reshape(b, s, a.shape[-1])
    vt16 = vt16.reshape(b, vt16.shape[0] // b, N_HEADS_A, VT_ROWS, vt16.shape[-1])
    oa = _attn_prompt(lv, key_feat, slope_feat, slope_rep, r3(q16), r3(k16), vt16, tq, tk)
    chunk = math.gcd(s, GLA_CHUNK)
    ob, s_p = _gla(r3(qb), r3(kb), r3(vb), r3(la), jnp.zeros((b, N_HEADS_B, DK_B, DV_B), F32), chunk)
    y_prompt = back(x_prompt, oa, ob, g)
    new_k_prompt = k32.reshape(1, b, s, N_HEADS_A, 2 * HEAD_DIM_A)
    new_v_prompt = v32.reshape(1, b, s, N_HEADS_A, V_DIM_A)

    q16, k32, k16, v32, v16, qb, kb, vb, la, g = front(x_sample)
    r3 = lambda a: a.reshape(db, t_new, a.shape[-1])
    pad16 = lambda a: jnp.pad(r3(a), ((0, 0), (0, 16 - t_new), (0, 0)))
    r4 = lambda a: a.reshape(db, t_new, N_HEADS_A, LANES)
    pad_new = lambda a: jnp.pad(r4(a), ((0, 0), (0, NEW_PAD - t_new), (0, 0), (0, 0)))
    cols = np.arange(COLS_S)
    head_info = np.zeros((3, N_HEADS_A, COLS_S), np.float32)
    head_info[0] = slopes2[:, None]
    head_info[1] = np.minimum(cols // 16, t_new - 1)[None, :]
    head_info[2] = ((cols % 8)[None, :] == np.arange(N_HEADS_A)[:, None]) & (cols < 16 * t_new)[None, :]
    ck = cache_k.reshape(depth * n_pool, PAGE_SIZE, N_HEADS_A, 2 * HEAD_DIM_A)
    cv = cache_v.reshape(depth * n_pool, PAGE_SIZE, N_HEADS_A, V_DIM_A)
    gp = math.gcd(page_table.shape[1], 8)
    rel = np.broadcast_to((slopes2[None, :] * np.arange(PAGE_SIZE, dtype=np.float32)[:, None])[:, :, None],
                          (PAGE_SIZE, N_HEADS_A, COLS_S))
    oa = _attn_sample(page_table, lv, jnp.asarray(head_info), jnp.asarray(rel), r4(q16.astype(F32)),
                      pad_new(k32), pad_new(v32), ck, cv, gp)
    ob, s_s = _gla(pad16(qb), pad16(kb), pad16(vb), pad16(la), state_gla[0], 16)
    y_sample = back(x_sample, oa, ob[:, :t_new], g)
    new_k_sample = k32.reshape(1, db, t_new, N_HEADS_A, 2 * HEAD_DIM_A)
    new_v_sample = v32.reshape(1, db, t_new, N_HEADS_A, V_DIM_A)

    return (y_prompt, y_sample, new_k_prompt, new_v_prompt, s_p[None], new_k_sample, new_v_sample, s_s[None])
```

```python
import functools
import math

import numpy as np
import jax
import jax.numpy as jnp
from jax import lax
from jax.experimental import pallas as pl
from jax.experimental.pallas import tpu as pltpu

D_MODEL = 1024
N_HEADS_A = 8
HEAD_DIM_A = 64
V_DIM_A = 2 * HEAD_DIM_A
SCALE_A = HEAD_DIM_A ** -0.5
N_HEADS_B = 4
DK_B = 128
DV_B = 256
ALPHA_RANK = 16
GLA_TAU = 16.0
GLA_CHUNK = 64
FFN_HIDDEN = 2816
EPS = 1e-6
PAGE_SIZE = 128
LAM_INIT = 0.8 - 0.6 * math.exp(-0.3 * 0)

QA_W = N_HEADS_A * 2 * HEAD_DIM_A
VA_W = N_HEADS_A * V_DIM_A
QB_W = N_HEADS_B * DK_B
VB_W = N_HEADS_B * DV_B
MAIN_W = 3 * QA_W + 2 * QB_W + VB_W
GATE_W = 2 * D_MODEL

LOG2E = math.log2(math.e)
VT_ROWS = V_DIM_A + 16

LANES = 128
V7X_VMEM_LIMIT_BYTES = 56 * 1024 * 1024

F32 = jnp.float32
BF16 = jnp.bfloat16
NEG_INF = float("-inf")


def _const_spec(shape):
    nd = len(shape)
    return pl.BlockSpec(shape, lambda *_: (0,) * nd, pipeline_mode=pl.Buffered(1))


def _params(semantics):
    return pltpu.CompilerParams(dimension_semantics=semantics, vmem_limit_bytes=V7X_VMEM_LIMIT_BYTES)


def _rms_rows(x, w):
    ms = jnp.mean(x * x, axis=-1, keepdims=True)
    return x * lax.rsqrt(ms + EPS) * w


def _sigmoid(x):
    return 1.0 / (1.0 + jnp.exp(-x))


def _log_sigmoid(x):
    return jnp.minimum(x, 0.0) - jnp.log1p(jnp.exp(-jnp.abs(x)))


def _nt_dot(a, b):
    return lax.dot_general(a, b, (((1,), (1,)), ((), ())), preferred_element_type=F32)


def _map_norm(z, w2):
    lane = lax.broadcasted_iota(jnp.int32, z.shape, 1)
    first = lane < HEAD_DIM_A
    sq = z * z
    s1 = jnp.sum(jnp.where(first, sq, 0.0), axis=-1, keepdims=True)
    s2 = jnp.sum(jnp.where(first, 0.0, sq), axis=-1, keepdims=True)
    ms = jnp.where(first, s1, s2) * (1.0 / HEAD_DIM_A)
    return z * lax.rsqrt(ms + EPS) * w2


def _front_kernel(x_ref, nw_ref, wm_ref, wa_ref, wg_ref, wup_ref, ba_ref, qw_ref, kw_ref,
                  q16_ref, k32_ref, k16_ref, v32_ref, vt16_ref, qb_ref, kb_ref, vb_ref, la_ref, g_ref):
    h = _rms_rows(x_ref[...], nw_ref[...]).astype(BF16)

    def proj(w):
        return jnp.dot(h, w, preferred_element_type=F32)

    zq = proj(wm_ref[:, 0:QA_W])
    for hd in range(N_HEADS_A):
        sl = slice(hd * LANES, (hd + 1) * LANES)
        q16_ref[:, sl] = (_map_norm(zq[:, sl], qw_ref[...]) * (SCALE_A * LOG2E)).astype(BF16)
    zk = proj(wm_ref[:, QA_W:2 * QA_W])
    for hd in range(N_HEADS_A):
        sl = slice(hd * LANES, (hd + 1) * LANES)
        kn = _map_norm(zk[:, sl], kw_ref[...])
        k32_ref[:, sl] = kn
        k16_ref[:, sl] = kn.astype(BF16)
    zv = proj(wm_ref[:, 2 * QA_W:3 * QA_W])
    v32_ref[...] = zv
    tm = zv.shape[0]
    vt = zv.T.reshape(N_HEADS_A, V_DIM_A, tm).astype(BF16)
    extra = lax.broadcasted_iota(jnp.int32, (N_HEADS_A, VT_ROWS - V_DIM_A, tm), 1)
    ones_row = jnp.where(extra == 0, 1.0, 0.0).astype(BF16)
    vt16_ref[0] = jnp.concatenate([vt, ones_row], axis=1)
    o = 3 * QA_W
    qb_ref[...] = proj(wm_ref[:, o:o + QB_W])
    kb_ref[...] = proj(wm_ref[:, o + QB_W:o + 2 * QB_W])
    vb_ref[...] = proj(wm_ref[:, o + 2 * QB_W:o + 2 * QB_W + VB_W])
    a_low = proj(wa_ref[...]).astype(BF16)
    logits = jnp.dot(a_low, wup_ref[...], preferred_element_type=F32) + ba_ref[...]
    la_ref[...] = _log_sigmoid(logits) * (1.0 / GLA_TAU)
    g_ref[...] = _sigmoid(proj(wg_ref[...]))


def _front(x2d, nw, wm, wa, wg, wup, ba, qw2, kw2, tm):
    m = x2d.shape[0]
    row = lambda w: pl.BlockSpec((tm, w), lambda i: (i, 0))
    outs = [(QA_W, BF16), (QA_W, F32), (QA_W, BF16), (VA_W, F32), None,
            (QB_W, F32), (QB_W, F32), (VB_W, F32), (QB_W, F32), (GATE_W, F32)]
    vt_shape = (m // tm, N_HEADS_A, VT_ROWS, tm)
    vt_spec = pl.BlockSpec((1, N_HEADS_A, VT_ROWS, tm), lambda i: (i, 0, 0, 0))
    return pl.pallas_call(
        _front_kernel,
        grid=(m // tm,),
        in_specs=[row(D_MODEL), _const_spec(nw.shape), _const_spec(wm.shape), _const_spec(wa.shape),
                  _const_spec(wg.shape), _const_spec(wup.shape), _const_spec(ba.shape),
                  _const_spec(qw2.shape), _const_spec(kw2.shape)],
        out_specs=[vt_spec if o is None else row(o[0]) for o in outs],
        out_shape=[jax.ShapeDtypeStruct(vt_shape, BF16) if o is None else jax.ShapeDtypeStruct((m, o[0]), o[1])
                   for o in outs],
        compiler_params=_params(("parallel",)),
        name="front",
    )(x2d, nw, wm, wa, wg, wup, ba, qw2, kw2)


def _lambda_from(lv_ref):
    lv = lv_ref[...]
    a = jnp.sum(lv[0:1] * lv[1:2], axis=-1, keepdims=True)
    b = jnp.sum(lv[2:3] * lv[3:4], axis=-1, keepdims=True)
    return jnp.exp(a) - jnp.exp(b) + LAM_INIT


ATTN_COL_BLOCK = 512
LOGITS_AHEAD = 2


def _attn_prompt_kernel(lv_ref, kf_ref, gf_ref, sr_ref, q_ref, k_ref, vt_ref, o_ref, qst_ref, m_ref, acc_ref,
                        pend_ref, *, tq, tk, tkv):
    i = pl.program_id(2)
    cb = ATTN_COL_BLOCK
    n_cb = 2 * tq // cb
    n_sub = tk // tkv
    n_full = (i * tq) // tk
    n_diag = tq // tk

    qt = q_ref[0].astype(F32).T
    row = lax.broadcasted_iota(jnp.int32, qt.shape, 0)
    qst_ref[0:LANES, 0:tq] = jnp.where(row < HEAD_DIM_A, qt, 0.0).astype(BF16)
    qst_ref[0:LANES, tq:2 * tq] = jnp.where(row < HEAD_DIM_A, 0.0, qt).astype(BF16)
    qst_ref[LANES:2 * LANES, :] = jnp.concatenate([gf_ref[0]] * (2 * tq // LANES), axis=1)
    m_ref[...] = jnp.full(m_ref.shape, NEG_INF, F32)
    acc_ref[...] = jnp.zeros(acc_ref.shape, F32)

    def logits(j, c):
        k = k_ref[0, pl.ds(pl.multiple_of(j * tk, tk), tk), :]
        k_aug = jnp.concatenate([k, kf_ref[...]], axis=1)
        s = jnp.dot(k_aug, qst_ref[:, c * cb:(c + 1) * cb], preferred_element_type=F32)
        return s.reshape(tk // 8, 8, cb)

    def online_tile(j, masked, next_j):
        vt = jnp.concatenate([vt_ref[0, j * n_sub + u, 0] for u in range(n_sub)], axis=1)
        tile_bias = jnp.concatenate([sr_ref[0] * (j * tk - i * tq).astype(F32)] * (cb // LANES), axis=1)
        pending = []
        for c in range(n_cb):
            cols = slice(c * cb, (c + 1) * cb)
            s = pend_ref[c] if c < LOGITS_AHEAD else pending.pop(0)
            ahead = c + LOGITS_AHEAD
            if ahead < n_cb:
                pending.append(logits(j, ahead))
            elif next_j is not None:
                pend_ref[ahead - n_cb] = logits(next_j, ahead - n_cb)
            if masked:
                kr = lax.broadcasted_iota(jnp.int32, s.shape, 0) * 8 + lax.broadcasted_iota(jnp.int32, s.shape, 1)
                qc = lax.broadcasted_iota(jnp.int32, s.shape, 2) + (c * cb) % tq
                s = jnp.where(kr + (j * tk - i * tq) <= qc, s, NEG_INF)
            m_old = m_ref[:, cols]
            m_new = jnp.maximum(m_old, jnp.max(jnp.max(s, axis=0), axis=0, keepdims=True) + tile_bias)
            alpha = jnp.exp2(m_old - m_new)
            p16 = jnp.exp2(s - (m_new - tile_bias)).reshape(tk, cb).astype(BF16)
            pv = jnp.dot(vt, p16, preferred_element_type=F32)
            acc = acc_ref[:, cols].reshape(VT_ROWS // 8, 8, cb) * alpha
            acc_ref[:, cols] = acc.reshape(VT_ROWS, cb) + pv
            m_ref[:, cols] = m_new

    for c in range(LOGITS_AHEAD):
        pend_ref[c] = logits(0, c)

    def body(j, carry):
        online_tile(j, False, j + 1)
        return carry

    lax.fori_loop(0, n_full, body, 0)
    for d in range(n_diag):
        online_tile(n_full + d, True, n_full + d + 1 if d + 1 < n_diag else None)

    lam = _lambda_from(lv_ref)
    l = acc_ref[V_DIM_A:V_DIM_A + 1, :]
    o1 = acc_ref[0:V_DIM_A, 0:tq] / l[:, 0:tq]
    o2 = acc_ref[0:V_DIM_A, tq:2 * tq] / l[:, tq:2 * tq]
    o_ref[0] = (o1 - lam * o2).T


def _attn_prompt(lv, key_feat, slope_feat, slope_rep, q16, k16, vt16, tq, tk):
    b, s, _ = q16.shape
    n_kv, tkv = vt16.shape[1], vt16.shape[-1]
    assert tq % tk == 0 and tk % tkv == 0 and (2 * tq) % ATTN_COL_BLOCK == 0
    return pl.pallas_call(
        functools.partial(_attn_prompt_kernel, tq=tq, tk=tk, tkv=tkv),
        grid=(b, N_HEADS_A, s // tq),
        in_specs=[pl.BlockSpec(lv.shape, lambda bi, h, i: (0, 0)),
                  pl.BlockSpec((tk, LANES), lambda bi, h, i: (0, 0)),
                  pl.BlockSpec((1, LANES, LANES), lambda bi, h, i: (h, 0, 0)),
                  pl.BlockSpec((1, 8, LANES), lambda bi, h, i: (h, 0, 0)),
                  pl.BlockSpec((1, tq, LANES), lambda bi, h, i: (bi, i, h)),
                  pl.BlockSpec((1, s, LANES), lambda bi, h, i: (bi, 0, h)),
                  pl.BlockSpec((1, n_kv, 1, VT_ROWS, tkv), lambda bi, h, i: (bi, 0, h, 0, 0))],
        out_specs=pl.BlockSpec((1, tq, LANES), lambda bi, h, i: (bi, i, h)),
        out_shape=jax.ShapeDtypeStruct((b, s, VA_W), F32),
        scratch_shapes=[pltpu.VMEM((2 * LANES, 2 * tq), BF16),
                        pltpu.VMEM((8, 2 * tq), F32),
                        pltpu.VMEM((VT_ROWS, 2 * tq), F32),
                        pltpu.VMEM((LOGITS_AHEAD, tk // 8, 8, ATTN_COL_BLOCK), F32)],
        compiler_params=_params(("parallel", "parallel", "arbitrary")),
        name="attn_prompt",
    )(lv, key_feat, slope_feat, slope_rep, q16, k16, vt16)


COLS_S = 128
NEW_PAD = 8
PAGES_AHEAD = 3


def _attn_sample_kernel(pt_ref, lv_ref, hv_ref, rel_ref, q_ref, kn_ref, vn_ref, *rest, gp, t_new, past_len):
    k_refs = rest[:gp]
    v_refs = rest[gp:2 * gp]
    o_ref, qr_ref, m_ref, l_ref, acc_ref = rest[2 * gp:]
    p = pl.program_id(1)
    slope = hv_ref[0]
    qpos = hv_ref[1] + float(past_len)
    own = hv_ref[2] > 0.5

    @pl.when(p == 0)
    def _():
        lane = lax.broadcasted_iota(jnp.int32, (N_HEADS_A, LANES), 1)
        tiles = []
        for t in range(t_new):
            qt = q_ref[0, t]
            tiles.append(jnp.where(lane < HEAD_DIM_A, qt, 0.0))
            tiles.append(jnp.where(lane < HEAD_DIM_A, 0.0, qt))
        tiles.append(jnp.zeros((COLS_S - 16 * t_new, LANES), F32))
        qr = jnp.concatenate(tiles, axis=0)
        zeros = jnp.zeros_like(qr)
        qr_ref[...] = jnp.concatenate([jnp.concatenate([qr, zeros], axis=1),
                                       jnp.concatenate([zeros, qr], axis=1)], axis=0).astype(BF16)
        m_ref[...] = jnp.full(m_ref.shape, NEG_INF, F32)
        l_ref[...] = jnp.zeros(l_ref.shape, F32)
        acc_ref[...] = jnp.zeros(acc_ref.shape, F32)

    def key_logits(k_ref_, n):
        k2 = k_ref_[0].reshape(n * N_HEADS_A, LANES).astype(BF16)
        half = k2.shape[0] // 2
        pair = _nt_dot(jnp.concatenate([k2[:half], k2[half:]], axis=1), qr_ref[...])
        return jnp.concatenate([pair[:, :COLS_S], pair[:, COLS_S:]], axis=0).reshape(n, N_HEADS_A, COLS_S)

    def update(t, v_ref_, rel, base, mask):
        n = t.shape[0]
        t = t + rel
        if mask is not None:
            t = jnp.where(mask, t, NEG_INF)
        cc = slope * (qpos - base)
        m_old = m_ref[...]
        m_new = jnp.maximum(m_old, jnp.max(t, axis=0) - cc)
        alpha = jnp.exp2(m_old - m_new)
        pr = jnp.exp2(t - jnp.where(own, m_new + cc, jnp.inf))
        l_ref[...] = alpha * l_ref[...] + jnp.sum(pr, axis=0)
        m_ref[...] = m_new
        pm = pr.reshape(n * N_HEADS_A, COLS_S).astype(BF16)
        v2 = v_ref_[0].reshape(n * N_HEADS_A, LANES).astype(BF16)
        pv = lax.dot_general(v2, pm, (((0,), (0,)), ((), ())), preferred_element_type=F32)
        alpha_row = jnp.sum(jnp.where(own, alpha, 0.0), axis=0, keepdims=True)
        acc_ref[...] = acc_ref[...] * alpha_row + pv

    pending = [key_logits(k_refs[g], PAGE_SIZE) for g in range(min(PAGES_AHEAD, gp))]
    for g in range(gp):
        t = pending.pop(0)
        if g + PAGES_AHEAD < gp:
            pending.append(key_logits(k_refs[g + PAGES_AHEAD], PAGE_SIZE))
        update(t, v_refs[g], rel_ref[...], ((p * gp + g) * PAGE_SIZE).astype(F32), None)

    @pl.when(p == pl.num_programs(1) - 1)
    def _():
        pos = lax.broadcasted_iota(jnp.int32, (NEW_PAD, N_HEADS_A, COLS_S), 0)
        mask = jnp.logical_and(pos < t_new, pos.astype(F32) + float(past_len) <= qpos)
        update(key_logits(kn_ref, NEW_PAD), vn_ref, rel_ref[0:NEW_PAD], float(past_len), mask)
        lam = _lambda_from(lv_ref)
        l_row = jnp.sum(jnp.where(own, l_ref[...], 0.0), axis=0, keepdims=True)
        used = jnp.sum(jnp.where(own, 1.0, 0.0), axis=0, keepdims=True) > 0.5
        o = (acc_ref[...] / jnp.where(used, l_row, 1.0)).T
        for t in range(t_new):
            o_ref[0, t] = o[16 * t:16 * t + 8, :] - lam * o[16 * t + 8:16 * t + 16, :]


def _attn_sample(page_table, lv, head_info, rel, q, kn, vn, cache_k, cache_v, gp):
    db, t_new = q.shape[:2]
    n_pages = page_table.shape[1]
    past_len = n_pages * PAGE_SIZE
    blk4 = lambda n: (1, n, N_HEADS_A, LANES)

    def page_spec(g):
        return pl.BlockSpec(blk4(PAGE_SIZE), lambda b, p, pt: (pt[b, p * gp + g], 0, 0, 0))

    grid_spec = pltpu.PrefetchScalarGridSpec(
        num_scalar_prefetch=1,
        grid=(db, n_pages // gp),
        in_specs=[pl.BlockSpec(lv.shape, lambda b, p, pt: (0, 0)),
                  pl.BlockSpec(head_info.shape, lambda b, p, pt: (0, 0, 0)),
                  pl.BlockSpec(rel.shape, lambda b, p, pt: (0, 0, 0), pipeline_mode=pl.Buffered(1)),
                  pl.BlockSpec(blk4(t_new), lambda b, p, pt: (b, 0, 0, 0)),
                  pl.BlockSpec(blk4(NEW_PAD), lambda b, p, pt: (b, 0, 0, 0)),
                  pl.BlockSpec(blk4(NEW_PAD), lambda b, p, pt: (b, 0, 0, 0))]
                 + [page_spec(g) for g in range(gp)] + [page_spec(g) for g in range(gp)],
        out_specs=pl.BlockSpec(blk4(t_new), lambda b, p, pt: (b, 0, 0, 0)),
        scratch_shapes=[pltpu.VMEM((2 * COLS_S, 2 * LANES), BF16),
                        pltpu.VMEM((N_HEADS_A, COLS_S), F32),
                        pltpu.VMEM((N_HEADS_A, COLS_S), F32),
                        pltpu.VMEM((V_DIM_A, COLS_S), F32)],
    )
    return pl.pallas_call(
        functools.partial(_attn_sample_kernel, gp=gp, t_new=t_new, past_len=past_len),
        grid_spec=grid_spec,
        out_shape=jax.ShapeDtypeStruct((db, t_new, N_HEADS_A, V_DIM_A), F32),
        compiler_params=_params(("parallel", "arbitrary")),
        name="attn_sample",
    )(page_table, lv, head_info, rel, q, kn, vn, *([cache_k] * gp), *([cache_v] * gp))


GLA_CHUNKS_PER_STEP = 2


def _split3(x):
    hi = x.astype(BF16)
    r1 = x - hi.astype(F32)
    mid = r1.astype(BF16)
    lo = (r1 - mid.astype(F32)).astype(BF16)
    return hi, mid, lo


def _gla_kernel(q_ref, k_ref, v_ref, la_ref, s0_ref, o_ref, sout_ref, st_ref, *, chunk):
    c = pl.program_id(1)

    @pl.when(c == 0)
    def _():
        for h in range(N_HEADS_B):
            st_ref[h] = s0_ref[0, h].T

    row = lax.broadcasted_iota(jnp.int32, (chunk, chunk), 0)
    col = lax.broadcasted_iota(jnp.int32, (chunk, chunk), 1)
    tri = row >= col
    tri16 = jnp.where(tri, 1.0, 0.0).astype(BF16)

    heads = range(N_HEADS_B)
    subs = range(q_ref.shape[1] // chunk)
    rows = [slice(n * chunk, (n + 1) * chunk) for n in subs]
    hs = [(n, h) for n in subs for h in heads]
    ks = [slice(h * DK_B, (h + 1) * DK_B) for h in heads]
    vs = [slice(h * DV_B, (h + 1) * DV_B) for h in heads]
    k = {(n, h): k_ref[0, rows[n], ks[h]] for n, h in hs}
    v = {(n, h): v_ref[0, rows[n], vs[h]] for n, h in hs}
    parts = {(n, h): _split3(la_ref[0, rows[n], ks[h]]) for n, h in hs}
    cum = {nh: sum(jnp.dot(tri16, part, preferred_element_type=F32) for part in parts[nh]) for nh in hs}
    last = {nh: cum[nh][chunk - 1:chunk, :] for nh in hs}
    qd = {(n, h): (q_ref[0, rows[n], ks[h]] * (DK_B ** -0.5) * jnp.exp(cum[n, h])).astype(BF16) for n, h in hs}
    kd = {nh: (k[nh] * jnp.exp(-cum[nh])).astype(BF16) for nh in hs}
    att = {nh: jnp.where(tri, _nt_dot(qd[nh], kd[nh]), 0.0).astype(BF16) for nh in hs}
    local = {nh: jnp.dot(att[nh], v[nh].astype(BF16), preferred_element_type=F32) for nh in hs}
    k2 = {nh: (k[nh] * jnp.exp(last[nh] - cum[nh])).astype(BF16) for nh in hs}
    st = [st_ref[h] for h in heads]
    for n in subs:
        carried = [_nt_dot(qd[n, h], st[h].astype(BF16)) for h in heads]
        for h in heads:
            o_ref[0, rows[n], vs[h]] = local[n, h] + carried[h]
        st = [st[h] * jnp.exp(last[n, h]) + jnp.dot(v[n, h].T.astype(BF16), k2[n, h], preferred_element_type=F32)
              for h in heads]
    for h in heads:
        st_ref[h] = st[h]

    @pl.when(c == pl.num_programs(1) - 1)
    def _():
        for h in range(N_HEADS_B):
            sout_ref[0, h] = st_ref[h].T


def _gla(qb, kb, vb, la, s0, chunk):
    b, t, _ = qb.shape
    per_step = GLA_CHUNKS_PER_STEP if (t // chunk) % GLA_CHUNKS_PER_STEP == 0 else 1
    nc = t // (chunk * per_step)
    tok = lambda w: pl.BlockSpec((1, chunk * per_step, w), lambda bi, c: (bi, c, 0))
    st_spec = pl.BlockSpec((1, N_HEADS_B, DK_B, DV_B), lambda bi, c: (bi, 0, 0, 0))
    return pl.pallas_call(
        functools.partial(_gla_kernel, chunk=chunk),
        grid=(b, nc),
        in_specs=[tok(QB_W), tok(QB_W), tok(VB_W), tok(QB_W), st_spec],
        out_specs=[tok(VB_W), st_spec],
        out_shape=[jax.ShapeDtypeStruct((b, t, VB_W), F32),
                   jax.ShapeDtypeStruct((b, N_HEADS_B, DK_B, DV_B), F32)],
        scratch_shapes=[pltpu.VMEM((N_HEADS_B, DV_B, DK_B), F32)],
        compiler_params=_params(("parallel", "arbitrary")),
        name="gla",
    )(qb, kb, vb, la, s0)


def _back_kernel(x_ref, oa_ref, ob_ref, g_ref, dn_ref, gn_ref, wo_ref, fn_ref, wg_ref, wu_ref, wd_ref, y_ref):
    def gated_mix(r):
        parts = []
        for h in range(N_HEADS_A):
            sl = slice(h * V_DIM_A, (h + 1) * V_DIM_A)
            oa = _rms_rows(oa_ref[r, sl], dn_ref[...]) * (1.0 - LAM_INIT)
            parts.append(g_ref[r, sl] * oa)
        mix_a = jnp.concatenate(parts, axis=-1)
        parts = []
        for h in range(N_HEADS_B):
            sl = slice(h * DV_B, (h + 1) * DV_B)
            ob = _rms_rows(ob_ref[r, sl], gn_ref[...])
            parts.append(g_ref[r, D_MODEL + h * DV_B:D_MODEL + (h + 1) * DV_B] * ob)
        return (mix_a + jnp.concatenate(parts, axis=-1)).astype(BF16)

    tm = x_ref.shape[0]
    halves = [slice(0, tm // 2), slice(tm // 2, tm)]
    dot = lambda a, w_ref: jnp.dot(a, w_ref[...], preferred_element_type=F32)
    mix = [gated_mix(r) for r in halves]
    x1 = [x_ref[r, :] + dot(mix[n], wo_ref) for n, r in enumerate(halves)]
    h2 = [_rms_rows(x, fn_ref[...]).astype(BF16) for x in x1]
    gate_up = [(dot(h, wg_ref), dot(h, wu_ref)) for h in h2]
    u = [(gt * _sigmoid(gt) * up).astype(BF16) for gt, up in gate_up]
    for n, r in enumerate(halves):
        y_ref[r, :] = x1[n] + dot(u[n], wd_ref)


def _back(x2d, oa, ob, g, dn, gn, wo, fn, wg, wu, wd, tm):
    m = x2d.shape[0]
    row = lambda w: pl.BlockSpec((tm, w), lambda i: (i, 0))
    return pl.pallas_call(
        _back_kernel,
        grid=(m // tm,),
        in_specs=[row(D_MODEL), row(VA_W), row(VB_W), row(GATE_W)]
                 + [_const_spec(a.shape) for a in (dn, gn, wo, fn, wg, wu, wd)],
        out_specs=row(D_MODEL),
        out_shape=jax.ShapeDtypeStruct((m, D_MODEL), F32),
        compiler_params=_params(("parallel",)),
        name="back",
    )(x2d, oa, ob, g, dn, gn, wo, fn, wg, wu, wd)


def _alibi_slopes(n):
    return np.array([2.0 ** (-8.0 * (i + 1) / n) for i in range(n)], np.float32)


def _alibi_features(slopes2, tk):
    assert tk <= 256 * 256
    rows = np.arange(tk)
    kf = np.zeros((tk, LANES), np.float32)
    kf[:, 0:3] = (256 * (rows // 256))[:, None]
    kf[:, 3:6] = (rows % 256)[:, None]
    to_bf16 = lambda x: x.astype(jnp.bfloat16).astype(np.float32)
    s1 = to_bf16(slopes2)
    s2 = to_bf16(slopes2 - s1)
    s3 = to_bf16(slopes2 - s1 - s2)
    sf = np.zeros((len(slopes2), LANES, LANES), np.float32)
    for r, term in enumerate((s1, s2, s3, s1, s2, s3)):
        sf[:, r, :] = term[:, None]
    return jnp.asarray(kf, BF16), jnp.asarray(sf, BF16)


def _pick_tile(n, pref):
    t = min(pref, n)
    assert n % t == 0, (n, t)
    return t


def kernel(x_prompt, x_sample, cache_k, cache_v, state_gla, page_table, attn_norm_w, w_in, w_alpha_up, b_alpha,
           q_norm_w, k_norm_w, lambda_q1, lambda_k1, lambda_q2, lambda_k2, diff_norm_w, gla_norm_w, w_out,
           ffn_norm_w, w_ffn_gate, w_ffn_up, w_ffn_down):
    depth = w_in.shape[0]
    assert depth == 1, "single-layer step"
    b, s, _ = x_prompt.shape
    db, t_new, _ = x_sample.shape
    n_pool = cache_k.shape[1]
    assert 16 * t_new <= COLS_S and t_new <= NEW_PAD

    w_in0 = w_in[0]
    wm = w_in0[:, :MAIN_W].astype(BF16)
    wa = jnp.pad(w_in0[:, MAIN_W:MAIN_W + ALPHA_RANK], ((0, 0), (0, LANES - ALPHA_RANK))).astype(BF16)
    wg_in = w_in0[:, MAIN_W + ALPHA_RANK:].astype(BF16)
    wup = jnp.pad(w_alpha_up[0], ((0, LANES - ALPHA_RANK), (0, 0))).astype(BF16)
    ba = b_alpha[0][None, :]
    nw = attn_norm_w[0][None, :]
    qw2 = jnp.tile(q_norm_w[0], 2)[None, :]
    kw2 = jnp.tile(k_norm_w[0], 2)[None, :]
    lv = jnp.stack([lambda_q1[0], lambda_k1[0], lambda_q2[0], lambda_k2[0]])
    dn = diff_norm_w[0][None, :]
    gn = gla_norm_w[0][None, :]
    fn = ffn_norm_w[0][None, :]
    wo = w_out[0].astype(BF16)
    wfg = w_ffn_gate[0].astype(BF16)
    wfu = w_ffn_up[0].astype(BF16)
    wfd = w_ffn_down[0].astype(BF16)
    slopes2 = (_alibi_slopes(N_HEADS_A).astype(np.float64) * LOG2E).astype(np.float32)

    def front(x3d):
        m = x3d.shape[0] * x3d.shape[1]
        return _front(x3d.reshape(m, D_MODEL), nw, wm, wa, wg_in, wup, ba, qw2, kw2, _pick_tile(m, 256))

    def back(x3d, oa, ob, g):
        m = x3d.shape[0] * x3d.shape[1]
        y = _back(x3d.reshape(m, D_MODEL), oa.reshape(m, VA_W), ob.reshape(m, VB_W), g, dn, gn, wo, fn,
                  wfg, wfu, wfd, _pick_tile(m, 256))
        return y.reshape(x3d.shape)

    q16, k32, k16, v32, vt16, qb, kb, vb, la, g = front(x_prompt)
    tq, tk = _pick_tile(s, 1024), _pick_tile(s, 1024)
    key_feat, slope_feat = _alibi_features(slopes2, tk)
    slope_rep = jnp.asarray(np.broadcast_to(slopes2[:, None, None], (N_HEADS_A, 8, LANES)))
    r3 = lambda a: a.reshape(b, s, a.shape[-1])
    vt16 = vt16.reshape(b, vt16.shape[0] // b, N_HEADS_A, VT_ROWS, vt16.shape[-1])
    oa = _attn_prompt(lv, key_feat, slope_feat, slope_rep, r3(q16), r3(k16), vt16, tq, tk)
    chunk = math.gcd(s, GLA_CHUNK)
    ob, s_p = _gla(r3(qb), r3(kb), r3(vb), r3(la), jnp.zeros((b, N_HEADS_B, DK_B, DV_B), F32), chunk)
    y_prompt = back(x_prompt, oa, ob, g)
    new_k_prompt = k32.reshape(1, b, s, N_HEADS_A, 2 * HEAD_DIM_A)
    new_v_prompt = v32.reshape(1, b, s, N_HEADS_A, V_DIM_A)

    q16, k32, k16, v32, v16, qb, kb, vb, la, g = front(x_sample)
    r3 = lambda a: a.reshape(db, t_new, a.shape[-1])
    pad16 = lambda a: jnp.pad(r3(a), ((0, 0), (0, 16 - t_new), (0, 0)))
    r4 = lambda a: a.reshape(db, t_new, N_HEADS_A, LANES)
    pad_new = lambda a: jnp.pad(r4(a), ((0, 0), (0, NEW_PAD - t_new), (0, 0), (0, 0)))
    cols = np.arange(COLS_S)
    head_info = np.zeros((3, N_HEADS_A, COLS_S), np.float32)
    head_info[0] = slopes2[:, None]
    head_info[1] = np.minimum(cols // 16, t_new - 1)[None, :]
    head_info[2] = ((cols % 8)[None, :] == np.arange(N_HEADS_A)[:, None]) & (cols < 16 * t_new)[None, :]
    ck = cache_k.reshape(depth * n_pool, PAGE_SIZE, N_HEADS_A, 2 * HEAD_DIM_A)
    cv = cache_v.reshape(depth * n_pool, PAGE_SIZE, N_HEADS_A, V_DIM_A)
    gp = math.gcd(page_table.shape[1], 8)
    rel = np.broadcast_to((slopes2[None, :] * np.arange(PAGE_SIZE, dtype=np.float32)[:, None])[:, :, None],
                          (PAGE_SIZE, N_HEADS_A, COLS_S))
    oa = _attn_sample(page_table, lv, jnp.asarray(head_info), jnp.asarray(rel), r4(q16.astype(F32)),
                      pad_new(k32), pad_new(v32), ck, cv, gp)
    ob, s_s = _gla(pad16(qb), pad16(kb), pad16(vb), pad16(la), state_gla[0], 16)
    y_sample = back(x_sample, oa, ob[:, :t_new], g)
    new_k_sample = k32.reshape(1, db, t_new, N_HEADS_A, 2 * HEAD_DIM_A)
    new_v_sample = v32.reshape(1, db, t_new, N_HEADS_A, V_DIM_A)

    return (y_prompt, y_sample, new_k_prompt, new_v_prompt, s_p[None], new_k_sample, new_v_sample, s_s[None])
```
